```python
import math
import jax, jax.numpy as jnp
from jax import lax
import numpy as np

D_MODEL = 1024
BATCH = 8
SEQ = 4096
DEPTH = 2
DEC_BATCH = 2
DEC_SEQ = 16384
PAST_LEN = 128

DN_HEADS = 8
DN_HEAD_DIM = 64
DN_WIDTH = DN_HEADS * DN_HEAD_DIM
DN_CONV = 5
DN_CHUNK = 64
DA_HEADS = 4
DA_HEAD_DIM = 64
DA_WIDTH = DA_HEADS * 2 * DA_HEAD_DIM
DA_QBLOCK = 128
ROPE_THETA = 10000.0
SG_GROUPS = 8
SG_WIDTH = 512
SG_GROUP_DIM = SG_WIDTH // SG_GROUPS
SG_CHUNK = 128
N_BRANCH = 3
BRANCH_WIDTH = 512
D_FF = 4 * D_MODEL
NORM_EPS = 1e-6
IN_SIZES = (DN_WIDTH, DN_WIDTH, DN_WIDTH, DN_WIDTH, 4 * DN_HEADS,
            DA_WIDTH, DA_WIDTH, DA_WIDTH, 2 * SG_WIDTH, N_BRANCH * D_MODEL)
IN_COLS = 4 * DN_WIDTH + 4 * DN_HEADS + 3 * DA_WIDTH + 2 * SG_WIDTH + N_BRANCH * D_MODEL

kernel_name = "hybrid_deltanet_diffattn_sgu_encoder"


def _split_points():
    pts, acc = [], 0
    for s in IN_SIZES[:-1]:
        acc += s
        pts.append(acc)
    return pts


def rmsnorm(x, g):
    xf = x.astype(jnp.float32)
    y = xf * lax.rsqrt(jnp.mean(xf * xf, -1, keepdims=True) + NORM_EPS)
    return (y * g.astype(jnp.float32)).astype(x.dtype)


def layernorm(x, g, b):
    xf = x.astype(jnp.float32)
    mu = jnp.mean(xf, -1, keepdims=True)
    xc = xf - mu
    y = xc * lax.rsqrt(jnp.mean(xc * xc, -1, keepdims=True) + NORM_EPS)
    return (y * g.astype(jnp.float32) + b.astype(jnp.float32)).astype(x.dtype)


def l2norm(x):
    return x * lax.rsqrt(jnp.sum(x * x, -1, keepdims=True) + 1e-6)


def rotary(x, pos):
    half = x.shape[-1] // 2
    inv = jnp.power(ROPE_THETA, -jnp.arange(half, dtype=jnp.float32) / half)
    ang = pos[:, None] * inv[None, :]
    cos = jnp.cos(ang)[None, :, None, :]
    sin = jnp.sin(ang)[None, :, None, :]
    xf = x.astype(jnp.float32)
    x1, x2 = xf[..., :half], xf[..., half:]
    return jnp.concatenate([x1 * cos - x2 * sin, x2 * cos + x1 * sin], -1).astype(x.dtype)


def centred_dwconv(x, w):
    k = w.shape[0]
    return lax.conv_general_dilated(
        x, w[:, None, :].astype(x.dtype), window_strides=(1,), padding=[(k // 2, k // 2)],
        dimension_numbers=('NWC', 'WIO', 'NWC'), feature_group_count=x.shape[-1])


def gated_delta_rule(q, k, v, g, beta):
    bz, s, h, dk = q.shape
    dv = v.shape[-1]
    c = DN_CHUNK
    n = s // c

    def chunks(t):
        t = t.astype(jnp.float32).reshape((bz, n, c, h) + t.shape[3:])
        return jnp.moveaxis(t, (1, 3), (0, 2))

    q, k, v = chunks(q), chunks(k), chunks(v)
    gc = jnp.cumsum(chunks(g), -1)
    beta = chunks(beta)
    kb = k * beta[..., None]
    diff = gc[..., :, None] - gc[..., None, :]
    idx = jnp.arange(c)
    lower_strict = idx[:, None] > idx[None, :]
    lower_incl = idx[:, None] >= idx[None, :]
    a_mat = jnp.einsum('nbhid,nbhjd->nbhij', kb, k) * jnp.exp(jnp.where(lower_strict, diff, -jnp.inf))
    eye = jnp.eye(c, dtype=jnp.float32)
    rhs = jnp.concatenate([v * beta[..., None], kb * jnp.exp(gc)[..., None]], -1)
    sol = lax.linalg.triangular_solve(eye + a_mat, rhs, left_side=True, lower=True)
    u, w = sol[..., :dv], sol[..., dv:]
    qk = jnp.einsum('nbhid,nbhjd->nbhij', q, k) * jnp.exp(jnp.where(lower_incl, diff, -jnp.inf))

    def step(state, xs):
        q_c, k_c, u_c, w_c, qk_c, g_c = xs
        v_new = u_c - jnp.einsum('bhcd,bhde->bhce', w_c, state)
        o = (jnp.einsum('bhcd,bhde->bhce', q_c * jnp.exp(g_c)[..., None], state)
             + jnp.einsum('bhij,bhje->bhie', qk_c, v_new))
        g_last = g_c[..., -1]
        state = (state * jnp.exp(g_last)[..., None, None]
                 + jnp.einsum('bhcd,bhce->bhde', k_c * jnp.exp(g_last[..., None] - g_c)[..., None], v_new))
        return state, o

    s0 = jnp.zeros((bz, h, dk, dv), jnp.float32)
    _, o = lax.scan(step, s0, (q, k, u, w, qk, gc))
    return jnp.moveaxis(o, (0, 2), (1, 3)).reshape(bz, s, h, dv)


def diff_attention(q, k, v, lam):
    bz, s = q.shape[:2]
    nb = s // DA_QBLOCK
    qb = jnp.moveaxis(q.reshape(bz, nb, DA_QBLOCK, 2 * DA_HEADS, DA_HEAD_DIM), 1, 0)
    kf = k.astype(jnp.float32)
    vf = v.astype(jnp.float32)
    scale = DA_HEAD_DIM ** -0.5

    def block(q_blk):
        sc = jnp.einsum('bqhd,bkhd->bhqk', q_blk.astype(jnp.float32), kf) * scale
        p = jax.nn.softmax(sc, -1).reshape(bz, DA_HEADS, 2, DA_QBLOCK, s)
        a = p[:, :, 0] - lam * p[:, :, 1]
        return jnp.einsum('bhqk,bkhe->bqhe', a, vf)

    o = lax.map(block, qb)
    return jnp.moveaxis(o, 0, 1).reshape(bz, s, DA_HEADS, 2 * DA_HEAD_DIM)


def spatial_gating(u, v, ln_g, ln_b, w_s, b_s):
    bz, s, _ = u.shape
    vn = layernorm(v, ln_g, ln_b).reshape(bz, s // SG_CHUNK, SG_CHUNK, SG_GROUPS, SG_GROUP_DIM)
    mixed = jnp.einsum('gpq,bnqgc->bnpgc', w_s, vn) + b_s.T[None, None, :, :, None]
    return u * mixed.reshape(bz, s, SG_WIDTH).astype(u.dtype)


def lambda_init(layer):
    return 0.8 - 0.6 * math.exp(-0.3 * layer)


def encoder_layer(x, lam_init, g_mix_pre, w_in, conv_w, dn_A_log, dn_dt_bias, dn_norm_g, da_lambda,
                  da_subln_g, sg_ln_g, sg_ln_b, sg_w, sg_b, w_branch, w_out, g_mix_post, g_mlp_pre,
                  w_up, w_down, g_mlp_post):
    bz, s, _ = x.shape
    f32 = jnp.float32
    h = rmsnorm(x, g_mix_pre)
    proj = h @ w_in
    (dn_q, dn_k, dn_v, dn_z, dn_gates, da_q, da_k, da_v, sg_uv, br_gate) = jnp.split(proj, _split_points(), -1)

    qkv = jax.nn.silu(centred_dwconv(jnp.concatenate([dn_q, dn_k, dn_v], -1), conv_w))
    q, k, v = (t.reshape(bz, s, DN_HEADS, DN_HEAD_DIM).astype(f32) for t in jnp.split(qkv, 3, -1))
    q = l2norm(q) * (DN_HEAD_DIM ** -0.5)
    k = l2norm(k)
    gates = dn_gates.astype(f32).reshape(bz, s, 4, DN_HEADS)
    g = -jnp.exp(dn_A_log.astype(f32)) * jax.nn.softplus(gates[:, :, 0:2] + dn_dt_bias.astype(f32))
    beta = jax.nn.sigmoid(gates[:, :, 2:4])
    rev = lambda t: jnp.flip(t, 1)
    o2 = gated_delta_rule(jnp.concatenate([q, rev(q)], 0), jnp.concatenate([k, rev(k)], 0),
                          jnp.concatenate([v, rev(v)], 0),
                          jnp.concatenate([g[:, :, 0], rev(g[:, :, 1])], 0),
                          jnp.concatenate([beta[:, :, 0], rev(beta[:, :, 1])], 0))
    o_dn = o2[:bz] + rev(o2[bz:])
    z = dn_z.astype(f32).reshape(bz, s, DN_HEADS, DN_HEAD_DIM)
    o_a = (rmsnorm(o_dn, dn_norm_g) * jax.nn.silu(z)).reshape(bz, s, DN_WIDTH)

    pos = jnp.arange(s, dtype=f32)
    qd = rotary(da_q.reshape(bz, s, 2 * DA_HEADS, DA_HEAD_DIM), pos)
    kd = rotary(da_k.reshape(bz, s, 2 * DA_HEADS, DA_HEAD_DIM), pos)
    vd = da_v.reshape(bz, s, DA_HEADS, 2 * DA_HEAD_DIM)
    lp = da_lambda.astype(f32)
    lam = jnp.exp(jnp.sum(lp[0] * lp[1])) - jnp.exp(jnp.sum(lp[2] * lp[3])) + lam_init
    o_da = diff_attention(qd, kd, vd, lam)
    o_b = (rmsnorm(o_da, da_subln_g) * (1.0 - lam_init)).reshape(bz, s, DA_WIDTH)

    u, vv = jnp.split(jax.nn.gelu(sg_uv), 2, -1)
    o_c = spatial_gating(u, vv, sg_ln_g, sg_ln_b, sg_w, sg_b)

    gate = jax.nn.sigmoid(br_gate.astype(f32)).reshape(bz, s, N_BRANCH, D_MODEL)
    dt = x.dtype
    merged = (gate[:, :, 0] * (o_a.astype(dt) @ w_branch[0]).astype(f32)
              + gate[:, :, 1] * (o_b.astype(dt) @ w_branch[1]).astype(f32)
              + gate[:, :, 2] * (o_c.astype(dt) @ w_branch[2]).astype(f32))
    mix = merged.astype(dt) @ w_out
    x = x + rmsnorm(mix, g_mix_post)

    hm = rmsnorm(x, g_mlp_pre)
    f = jnp.square(jax.nn.relu(hm @ w_up)) @ w_down
    return x + rmsnorm(f, g_mlp_post)


def run_trunk(x, g_mix_pre, w_in, conv_w, dn_A_log, dn_dt_bias, dn_norm_g, da_lambda, da_subln_g,
              sg_ln_g, sg_ln_b, sg_w, sg_b, w_branch, w_out, g_mix_post, g_mlp_pre, w_up, w_down, g_mlp_post):
    for l in range(DEPTH):
        x = encoder_layer(x, lambda_init(l), g_mix_pre[l], w_in[l], conv_w[l], dn_A_log[l], dn_dt_bias[l],
                          dn_norm_g[l], da_lambda[l], da_subln_g[l], sg_ln_g[l], sg_ln_b[l], sg_w[l], sg_b[l],
                          w_branch[l], w_out[l], g_mix_post[l], g_mlp_pre[l], w_up[l], w_down[l], g_mlp_post[l])
    return x


def setup_inputs(seed: int = 0) -> dict:
    key = jax.random.key(seed)
    ks = jax.random.split(key, 24)
    nrm = lambda k, shp, sc: jax.random.normal(k, shp, jnp.float32) * sc
    gain = lambda k, shp: 1.0 + 0.1 * jax.random.normal(k, shp, jnp.float32)
    dt = jnp.exp(jax.random.uniform(ks[6], (DEPTH, 2, DN_HEADS), jnp.float32,
                                    minval=math.log(1e-3), maxval=math.log(0.1)))
    return {
        "x_prompt": nrm(ks[0], (BATCH, SEQ, D_MODEL), 1.0),
        "x_sample": nrm(ks[1], (DEC_BATCH, DEC_SEQ, D_MODEL), 1.0),
        "g_mix_pre": gain(ks[2], (DEPTH, D_MODEL)),
        "w_in": nrm(ks[3], (DEPTH, D_MODEL, IN_COLS), D_MODEL ** -0.5),
        "conv_w": nrm(ks[4], (DEPTH, DN_CONV, 3 * DN_WIDTH), DN_CONV ** -0.5),
        "dn_A_log": jnp.log(jax.random.uniform(ks[5], (DEPTH, 2, DN_HEADS), jnp.float32, minval=1.0, maxval=16.0)),
        "dn_dt_bias": dt + jnp.log(-jnp.expm1(-dt)),
        "dn_norm_g": gain(ks[7], (DEPTH, DN_HEAD_DIM)),
        "da_lambda": nrm(ks[8], (DEPTH, 4, DA_HEAD_DIM), 0.1),
        "da_subln_g": gain(ks[9], (DEPTH, 2 * DA_HEAD_DIM)),
        "sg_ln_g": gain(ks[10], (DEPTH, SG_WIDTH)),
        "sg_ln_b": nrm(ks[11], (DEPTH, SG_WIDTH), 0.02),
        "sg_w": nrm(ks[12], (DEPTH, SG_GROUPS, SG_CHUNK, SG_CHUNK), SG_CHUNK ** -0.5),
        "sg_b": gain(ks[13], (DEPTH, SG_GROUPS, SG_CHUNK)),
        "w_branch": nrm(ks[14], (DEPTH, N_BRANCH, BRANCH_WIDTH, D_MODEL), BRANCH_WIDTH ** -0.5),
        "w_out": nrm(ks[15], (DEPTH, D_MODEL, D_MODEL), D_MODEL ** -0.5),
        "g_mix_post": gain(ks[16], (DEPTH, D_MODEL)),
        "g_mlp_pre": gain(ks[17], (DEPTH, D_MODEL)),
        "w_up": nrm(ks[18], (DEPTH, D_MODEL, D_FF), D_MODEL ** -0.5),
        "w_down": nrm(ks[19], (DEPTH, D_FF, D_MODEL), D_FF ** -0.5),
        "g_mlp_post": gain(ks[20], (DEPTH, D_MODEL)),
    }


def reference(x_prompt, x_sample, g_mix_pre, w_in, conv_w, dn_A_log, dn_dt_bias, dn_norm_g, da_lambda,
              da_subln_g, sg_ln_g, sg_ln_b, sg_w, sg_b, w_branch, w_out, g_mix_post, g_mlp_pre, w_up, w_down,
              g_mlp_post):
    y_prompt = run_trunk(x_prompt, g_mix_pre, w_in, conv_w, dn_A_log, dn_dt_bias, dn_norm_g, da_lambda,
                         da_subln_g, sg_ln_g, sg_ln_b, sg_w, sg_b, w_branch, w_out, g_mix_post, g_mlp_pre,
                         w_up, w_down, g_mlp_post)
    y_sample = run_trunk(x_sample, g_mix_pre, w_in, conv_w, dn_A_log, dn_dt_bias, dn_norm_g, da_lambda,
                         da_subln_g, sg_ln_g, sg_ln_b, sg_w, sg_b, w_branch, w_out, g_mix_post, g_mlp_pre,
                         w_up, w_down, g_mlp_post)
    return (y_prompt, y_sample)
```

```python
import functools
import math

import numpy as np
import jax
import jax.numpy as jnp
from jax import lax
from jax.experimental import pallas as pl
from jax.experimental.pallas import tpu as pltpu

F32 = jnp.float32
BF16 = jnp.bfloat16

D_MODEL = 1024
DN_HEADS = 8
DN_HEAD_DIM = 64
DN_WIDTH = DN_HEADS * DN_HEAD_DIM
DN_CONV = 5
DN_CHUNK = 64
DA_HEADS = 4
DA_HEAD_DIM = 64
DA_WIDTH = DA_HEADS * 2 * DA_HEAD_DIM
ROPE_THETA = 10000.0
SG_GROUPS = 8
SG_WIDTH = 512
SG_GROUP_DIM = SG_WIDTH // SG_GROUPS
SG_CHUNK = 128
N_BRANCH = 3
NORM_EPS = 1e-6
L2_EPS = 1e-6

LANES = 128
GATE_LANES = LANES
DN_GROUP = 4
DN_GROUP_LANES = DN_GROUP * DN_HEAD_DIM
DN_NGROUPS = DN_HEADS // DN_GROUP
HALO_ROWS = 16

VMEM_LIMIT = 56 * 2**20


def _cparams(*sem):
    return pltpu.CompilerParams(dimension_semantics=sem, vmem_limit_bytes=VMEM_LIMIT)


def _const_spec(shape):
    nd = len(shape)
    return pl.BlockSpec(shape, lambda *_: (0,) * nd, pipeline_mode=pl.Buffered(1))


def _dot(a, b):
    return jnp.dot(a, b, preferred_element_type=F32)


def _dot_nt(a, b):
    return lax.dot_general(a, b, (((1,), (1,)), ((), ())), preferred_element_type=F32)


def _split2(x):
    hi = x.astype(BF16)
    lo = (x - hi.astype(F32)).astype(BF16)
    return hi, lo


def _rms(x, g):
    return x * lax.rsqrt(jnp.mean(x * x, axis=-1, keepdims=True) + NORM_EPS) * g


def _rope_body(inv_ref, cos_ref, sin_ref):
    tb = cos_ref.shape[0]
    row = lax.broadcasted_iota(jnp.int32, (tb, LANES), 0) + pl.program_id(0) * tb
    lane = lax.broadcasted_iota(jnp.int32, (tb, LANES), 1)
    ang = row.astype(F32) * inv_ref[...]
    half = DA_HEAD_DIM // 2
    sign = jnp.where(lane % DA_HEAD_DIM < half, -1.0, 1.0)
    cos_ref[...] = jnp.cos(ang)
    sin_ref[...] = jnp.sin(ang) * sign


def _rope_tables(seq):
    half = DA_HEAD_DIM // 2
    inv = jnp.power(ROPE_THETA, -jnp.arange(half, dtype=F32) / half)
    inv = jnp.tile(inv, LANES // half)[None, :]
    tb = min(seq, 1024)
    return pl.pallas_call(
        _rope_body,
        grid=(seq // tb,),
        in_specs=[pl.BlockSpec((1, LANES), lambda i: (0, 0))],
        out_specs=[pl.BlockSpec((tb, LANES), lambda i: (i, 0))] * 2,
        out_shape=[jax.ShapeDtypeStruct((seq, LANES), F32)] * 2,
        name="rope_tables",
        compiler_params=_cparams("parallel"),
    )(inv)


def _inproj_body(x_ref, g_ref, cos_ref, sin_ref, wqkv, wz, wg, wq, wk, wv, wsg, wbr,
                 oqkv, oz, og, oq, ok, ov, osg, obr, h_scr):
    h_scr[...] = _rms(x_ref[...], g_ref[...]).astype(BF16)
    tm = x_ref.shape[0]
    chunk = 512

    def proj(w_ref, o_ref, epilogue=None):
        width = w_ref.shape[1]
        step = min(chunk, width)
        for c0 in range(0, width, step):
            y = _dot(h_scr[...], w_ref[:, c0:c0 + step])
            if epilogue is not None:
                y = epilogue(y)
            o_ref[:, c0:c0 + step] = y.astype(o_ref.dtype)

    reps = DA_WIDTH // LANES
    cos = jnp.concatenate([cos_ref[...]] * reps, axis=1)
    sin = jnp.concatenate([sin_ref[...]] * reps, axis=1)
    lane = lax.broadcasted_iota(jnp.int32, (tm, DA_WIDTH), 1)
    first_half = lane % DA_HEAD_DIM < DA_HEAD_DIM // 2
    half = DA_HEAD_DIM // 2

    def rope(scale):
        def f(y):
            up = pltpu.roll(y, DA_WIDTH - half, axis=1)
            dn = pltpu.roll(y, half, axis=1)
            out = y * cos + jnp.where(first_half, up, dn) * sin
            return out * scale if scale != 1.0 else out
        return f

    proj(wqkv, oqkv)
    proj(wz, oz)
    proj(wg, og)
    proj(wq, oq, rope(DA_HEAD_DIM ** -0.5))
    proj(wk, ok, rope(1.0))
    proj(wv, ov)
    proj(wsg, osg)
    proj(wbr, obr)


def _inproj(x, g, cos, sin, ws, seq, tm):
    t = x.shape[0]
    row = lambda w: pl.BlockSpec((tm, w), lambda i: (i, 0))
    tiles_per_seq = seq // tm
    tab = pl.BlockSpec((tm, LANES), lambda i: (i % tiles_per_seq, 0))
    widths = [w.shape[1] for w in ws]
    dtypes = [BF16, BF16, F32, BF16, BF16, BF16, BF16, BF16]
    return pl.pallas_call(
        _inproj_body,
        grid=(t // tm,),
        in_specs=[row(D_MODEL), _const_spec((1, D_MODEL)), tab, tab] + [_const_spec(w.shape) for w in ws],
        out_specs=[row(w) for w in widths],
        out_shape=[jax.ShapeDtypeStruct((t, w), d) for w, d in zip(widths, dtypes)],
        scratch_shapes=[pltpu.VMEM((tm, D_MODEL), BF16)],
        name="in_proj",
        compiler_params=_cparams("parallel"),
    )(x, g, cos, sin, *ws)


def _dnprep_body(x_ref, xp_ref, xn_ref, gts_ref, cw_ref, alog_ref, dtb_ref, ones_ref, tri_ref,
                 q_ref, k_ref, v_ref, go_ref, ext_scr, *, tiles_per_seq):
    i = pl.program_id(0)
    tm = x_ref.shape[0]
    pos = i % tiles_per_seq
    half = HALO_ROWS // 2
    keep_prev = (pos != 0).astype(F32)
    keep_next = (pos != tiles_per_seq - 1).astype(F32)
    ext_scr[0:half, :] = xp_ref[...].astype(F32)[half:, :] * keep_prev
    ext_scr[half:half + tm, :] = x_ref[...].astype(F32)
    ext_scr[half + tm:HALO_ROWS + tm, :] = xn_ref[...].astype(F32)[:half, :] * keep_next

    ones = ones_ref[...]
    pad = DN_CONV // 2
    for part, o_ref in enumerate((q_ref, k_ref, v_ref)):
        ls = slice(DN_WIDTH * part, DN_WIDTH * (part + 1))
        acc = None
        for j in range(DN_CONV):
            off = half - pad + j
            term = ext_scr[off:off + tm, ls] * cw_ref[j:j + 1, ls]
            acc = term if acc is None else acc + term
        y = acc * jax.nn.sigmoid(acc)
        if part < 2:
            hi, lo = _split2(y * y)
            ss = _dot(hi, ones) + _dot(lo, ones)
            y = y * lax.rsqrt(ss + L2_EPS)
            if part == 0:
                y = y * (DN_HEAD_DIM ** -0.5)
        o_ref[...] = y.astype(o_ref.dtype)

    gl = gts_ref[...]
    lane = lax.broadcasted_iota(jnp.int32, (DN_CHUNK, GATE_LANES), 1)
    z = gl + dtb_ref[...]
    softplus = jnp.maximum(z, 0.0) + jnp.log1p(jnp.exp(-jnp.abs(z)))
    g = -jnp.exp(alog_ref[...]) * softplus
    beta = jax.nn.sigmoid(gl)
    tri = tri_ref[...]
    for c in range(tm // DN_CHUNK):
        rows = slice(c * DN_CHUNK, (c + 1) * DN_CHUNK)
        gs = g[rows]
        hi = gs.astype(BF16)
        r1 = gs - hi.astype(F32)
        mid = r1.astype(BF16)
        lo = (r1 - mid.astype(F32)).astype(BF16)
        cs = _dot(tri, jnp.concatenate([hi, mid, lo], axis=1))
        cs = cs[:, :GATE_LANES] + cs[:, GATE_LANES:2 * GATE_LANES] + cs[:, 2 * GATE_LANES:]
        fwd, bwd = cs[:DN_CHUNK], cs[DN_CHUNK:]
        go_ref[rows, :] = jnp.where(lane < DN_HEADS, fwd, jnp.where(lane < 2 * DN_HEADS, bwd, beta[rows]))


def _dnprep(qkv, gates, conv_w, alog, dtb, ones_bd, tri, seq, tm):
    t = qkv.shape[0]
    cw = qkv.shape[1]
    hb = tm // HALO_ROWS
    nhalo = t // HALO_ROWS
    tiles_per_seq = seq // tm
    out = pl.BlockSpec((tm, DN_WIDTH), lambda i: (i, 0))
    return pl.pallas_call(
        functools.partial(_dnprep_body, tiles_per_seq=tiles_per_seq),
        grid=(t // tm,),
        in_specs=[
            pl.BlockSpec((tm, cw), lambda i: (i, 0)),
            pl.BlockSpec((HALO_ROWS, cw), lambda i: (jnp.maximum(i * hb - 1, 0), 0)),
            pl.BlockSpec((HALO_ROWS, cw), lambda i: (jnp.minimum((i + 1) * hb, nhalo - 1), 0)),
            pl.BlockSpec((tm, GATE_LANES), lambda i: (i, 0)),
            _const_spec(conv_w.shape), _const_spec(alog.shape), _const_spec(dtb.shape),
            _const_spec(ones_bd.shape), _const_spec(tri.shape),
        ],
        out_specs=[out, out, out, pl.BlockSpec((tm, GATE_LANES), lambda i: (i, 0))],
        out_shape=[jax.ShapeDtypeStruct((t, DN_WIDTH), BF16)] * 3 + [jax.ShapeDtypeStruct((t, GATE_LANES), F32)],
        scratch_shapes=[pltpu.VMEM((tm + HALO_ROWS, cw), F32)],
        name="dn_prep",
        compiler_params=_cparams("parallel"),
    )(qkv, qkv, qkv, gates, conv_w, alog, dtb, ones_bd, tri)


def _dn_chunk(q, k, v, gc, beta, s_ref, bd, eye, strict, last_row):
    def blockdiag(x16):
        return jnp.concatenate([x16] * DN_GROUP, axis=0) * bd

    kf = k.astype(F32)
    kb = kf * beta
    pair = _dot_nt(jnp.concatenate([kb.astype(BF16), q], axis=0), blockdiag(k))
    gc_row = jnp.sum(gc * eye, axis=0, keepdims=True)
    dec = jnp.exp(jnp.where(strict > 0.5, gc - gc_row, -jnp.inf))
    a = pair[:DN_CHUNK] * dec
    qk = pair[DN_CHUNK:] * (dec + eye)

    b = -a
    qm = b
    b16 = b.astype(BF16)
    b = _dot(b16, blockdiag(b16))
    levels = int(math.log2(DN_CHUNK)) - 1
    for lvl in range(1, levels + 1):
        b16 = b.astype(BF16)
        if lvl < levels:
            r = _dot(jnp.concatenate([qm.astype(BF16), b16], axis=0), blockdiag(b16))
            qm = qm + b + r[:DN_CHUNK]
            b = r[DN_CHUNK:]
        else:
            qm = qm + b + _dot(qm.astype(BF16), blockdiag(b16))

    egc = jnp.exp(gc)
    vb = v.astype(F32) * beta
    kbg = kb * egc
    rhs = jnp.concatenate([blockdiag(vb.astype(BF16)), blockdiag(kbg.astype(BF16))], axis=1)
    uw = jnp.concatenate([vb, kbg], axis=1) + _dot(qm.astype(BF16), rhs)
    u, w = uw[:, :DN_GROUP_LANES], uw[:, DN_GROUP_LANES:]

    state = s_ref[...]
    qg = q.astype(F32) * egc
    ws = _dot(jnp.concatenate([w.astype(BF16), qg.astype(BF16)], axis=0), state.astype(BF16))
    v_new = (u - ws[:DN_CHUNK]).astype(BF16)
    o = ws[DN_CHUNK:] + _dot(qk.astype(BF16), blockdiag(v_new))
    g_last = gc[last_row:last_row + 1, :]
    kdec = kf * jnp.exp(g_last - gc)
    upd = _dot(kdec.T.astype(BF16), v_new)
    s_ref[...] = (state * jnp.exp(g_last) + upd) * bd.astype(F32)
    return o


def _dn_body(qf, kf, vf, gf, qb, kb, vb, gb, exp_ref, bd_ref, of_ref, ob_ref, state, gexp, *, cb):
    i = pl.program_id(1)

    @pl.when(i == 0)
    def _():
        state[...] = jnp.zeros_like(state)

    for d, g_ref in enumerate((gf, gb)):
        hi, lo = _split2(g_ref[...])
        e = _dot(jnp.concatenate([hi, lo], axis=0), exp_ref[d])
        rows = g_ref.shape[0]
        gexp[d] = e[:rows] + e[rows:]

    bd = bd_ref[...]
    ri = lax.broadcasted_iota(jnp.int32, (DN_CHUNK, DN_GROUP_LANES), 0)
    cj = lax.broadcasted_iota(jnp.int32, (DN_CHUNK, DN_GROUP_LANES), 1) % DN_CHUNK
    eye = (ri == cj).astype(F32)
    lower = (ri > cj).astype(F32)
    upper = (ri < cj).astype(F32)
    gl = DN_GROUP_LANES

    def body(c, carry):
        for d, (q_ref, k_ref, v_ref, o_ref) in enumerate(((qf, kf, vf, of_ref), (qb, kb, vb, ob_ref))):
            cc = c if d == 0 else cb - 1 - c
            rows = pl.ds(pl.multiple_of(cc * DN_CHUNK, DN_CHUNK), DN_CHUNK)
            for grp in range(DN_NGROUPS):
                ls = slice(grp * gl, (grp + 1) * gl)
                gc = gexp[d, rows, grp * gl:(grp + 1) * gl]
                beta = gexp[d, rows, (DN_NGROUPS + grp) * gl:(DN_NGROUPS + grp + 1) * gl]
                o = _dn_chunk(q_ref[rows, ls], k_ref[rows, ls], v_ref[rows, ls], gc, beta,
                              state.at[d * DN_NGROUPS + grp], bd, eye,
                              lower if d == 0 else upper, DN_CHUNK - 1 if d == 0 else 0)
                o_ref[rows, ls] = o.astype(o_ref.dtype)
        return carry

    lax.fori_loop(0, cb, body, 0)


def _dn_main(q, k, v, gates, expand, bd, batch, seq, cb):
    t = q.shape[0]
    rows = cb * DN_CHUNK
    nb = seq // rows
    fwd = lambda w: pl.BlockSpec((rows, w), lambda b, i: (b * nb + i, 0))
    bwd = lambda w: pl.BlockSpec((rows, w), lambda b, i: (b * nb + nb - 1 - i, 0))
    return pl.pallas_call(
        functools.partial(_dn_body, cb=cb),
        grid=(batch, nb),
        in_specs=[fwd(DN_WIDTH)] * 3 + [fwd(GATE_LANES)] + [bwd(DN_WIDTH)] * 3 + [bwd(GATE_LANES)]
                 + [_const_spec(expand.shape), _const_spec(bd.shape)],
        out_specs=[fwd(DN_WIDTH), bwd(DN_WIDTH)],
        out_shape=[jax.ShapeDtypeStruct((t, DN_WIDTH), BF16)] * 2,
        scratch_shapes=[pltpu.VMEM((2 * DN_NGROUPS, DN_GROUP_LANES, DN_GROUP_LANES), F32),
                        pltpu.VMEM((2, rows, 2 * DN_NGROUPS * DN_GROUP_LANES), F32)],
        name="dn_delta_rule",
        compiler_params=_cparams("arbitrary", "arbitrary"),
    )(q, k, v, gates, q, k, v, gates, expand, bd)


def _da_body(q_ref, k_ref, v_ref, lam_ref, g_ref, o_ref, q2_scr, *, tk, lam_init):
    tq = q_ref.shape[0]
    seq = k_ref.shape[0]
    lane = lax.broadcasted_iota(jnp.int32, (tq, LANES), 1)
    q = q_ref[...]
    zero = jnp.zeros_like(q)
    q2_scr[0:tq, :] = jnp.where(lane < DA_HEAD_DIM, q, zero)
    q2_scr[tq:2 * tq, :] = jnp.where(lane < DA_HEAD_DIM, zero, q)

    def step(j, carry):
        m, l, acc = carry
        rows = pl.ds(pl.multiple_of(j * tk, tk), tk)
        s = _dot_nt(q2_scr[...], k_ref[rows, :])
        m_new = jnp.maximum(m, jnp.max(s, axis=1, keepdims=True))
        p = jnp.exp(s - m_new)
        alpha = jnp.exp(m - m_new)
        l = alpha * l + jnp.sum(p, axis=1, keepdims=True)
        acc = alpha * acc + _dot(p.astype(BF16), v_ref[rows, :])
        return m_new, l, acc

    init = (jnp.full((2 * tq, 1), -jnp.inf, F32), jnp.zeros((2 * tq, 1), F32), jnp.zeros((2 * tq, LANES), F32))
    _, l, acc = lax.fori_loop(0, seq // tk, step, init)
    o = acc / l
    lp = lam_ref[...]
    lam = (jnp.exp(jnp.sum(lp[0:1] * lp[1:2], axis=1, keepdims=True))
           - jnp.exp(jnp.sum(lp[2:3] * lp[3:4], axis=1, keepdims=True)) + lam_init)
    od = o[:tq] - lam * o[tq:]
    o_ref[...] = (_rms(od, g_ref[...]) * (1.0 - lam_init)).astype(o_ref.dtype)


def _diff_attn(q, k, v, lam, g, batch, seq, tq, tk, lam_init):
    t = q.shape[0]
    nq = seq // tq
    qspec = pl.BlockSpec((tq, LANES), lambda b, h, i: (b * nq + i, h))
    kspec = pl.BlockSpec((seq, LANES), lambda b, h, i: (b, h))
    return pl.pallas_call(
        functools.partial(_da_body, tk=tk, lam_init=lam_init),
        grid=(batch, DA_HEADS, nq),
        in_specs=[qspec, kspec, kspec,
                  pl.BlockSpec(lam.shape, lambda b, h, i: (0, 0)),
                  pl.BlockSpec(g.shape, lambda b, h, i: (0, 0))],
        out_specs=qspec,
        out_shape=jax.ShapeDtypeStruct((t, DA_WIDTH), BF16),
        scratch_shapes=[pltpu.VMEM((2 * tq, LANES), BF16)],
        name="diff_attn",
        compiler_params=_cparams("parallel", "parallel", "arbitrary"),
    )(q, k, v, lam, g)


def _gelu_tanh(x):
    return 0.5 * x * (1.0 + jnp.tanh(math.sqrt(2.0 / math.pi) * (x + 0.044715 * (x * x * x))))


def _sgu_body(uv_ref, lng_ref, lnb_ref, w_ref, bias_ref, bd_ref, o_ref):
    tm = uv_ref.shape[0]
    w = w_ref[...]
    bd = bd_ref[...]
    for c in range(tm // SG_CHUNK):
        rows = slice(c * SG_CHUNK, (c + 1) * SG_CHUNK)
        uv = _gelu_tanh(uv_ref[rows, :].astype(F32))
        u, v = uv[:, :SG_WIDTH], uv[:, SG_WIDTH:]
        mu = jnp.mean(v, axis=-1, keepdims=True)
        vc = v - mu
        vn = vc * lax.rsqrt(jnp.mean(vc * vc, axis=-1, keepdims=True) + NORM_EPS) * lng_ref[...] + lnb_ref[...]
        vbd = jnp.concatenate([vn.astype(BF16)] * SG_GROUPS, axis=0) * bd
        mixed = _dot(w, vbd) + bias_ref[...]
        o_ref[rows, :] = (u * mixed).astype(o_ref.dtype)


def _sgu(uv, ln_g, ln_b, wcat, bias, bd, tm):
    t = uv.shape[0]
    return pl.pallas_call(
        _sgu_body,
        grid=(t // tm,),
        in_specs=[pl.BlockSpec((tm, 2 * SG_WIDTH), lambda i: (i, 0)),
                  _const_spec(ln_g.shape), _const_spec(ln_b.shape), _const_spec(wcat.shape),
                  _const_spec(bias.shape), _const_spec(bd.shape)],
        out_specs=pl.BlockSpec((tm, SG_WIDTH), lambda i: (i, 0)),
        out_shape=jax.ShapeDtypeStruct((t, SG_WIDTH), BF16),
        name="spatial_gating",
        compiler_params=_cparams("parallel"),
    )(uv, ln_g, ln_b, wcat, bias, bd)


def _post_body(x_ref, of_ref, ob_ref, z_ref, oatt_ref, osg_ref, br_ref, dng_ref, ones_ref,
               wb_ref, wout_ref, gpost_ref, gpre2_ref, wup_ref, wdown_ref, gpost2_ref, y_ref):
    od = of_ref[...].astype(F32) + ob_ref[...].astype(F32)
    hi, lo = _split2(od * od)
    ones = ones_ref[...]
    ms = (_dot(hi, ones) + _dot(lo, ones)) * (1.0 / DN_HEAD_DIM)
    z = z_ref[...].astype(F32)
    o_a = od * lax.rsqrt(ms + NORM_EPS) * dng_ref[...] * (z * jax.nn.sigmoid(z))

    merged = None
    for idx, o_i in enumerate((o_a.astype(BF16), oatt_ref[...], osg_ref[...])):
        gate = jax.nn.sigmoid(br_ref[:, idx * D_MODEL:(idx + 1) * D_MODEL].astype(F32))
        term = gate * _dot(o_i, wb_ref[idx])
        merged = term if merged is None else merged + term
    mix = _dot(merged.astype(BF16), wout_ref[...])
    x1 = x_ref[...] + _rms(mix, gpost_ref[...])

    hm = _rms(x1, gpre2_ref[...]).astype(BF16)
    d_ff = wup_ref.shape[1]
    chunk = 1024
    f = None
    for c0 in range(0, d_ff, chunk):
        hid = jnp.maximum(_dot(hm, wup_ref[:, c0:c0 + chunk]), 0.0)
        term = _dot((hid * hid).astype(BF16), wdown_ref[c0:c0 + chunk, :])
        f = term if f is None else f + term
    y_ref[...] = x1 + _rms(f, gpost2_ref[...])


def _post(x, o_f, o_b, z, o_att, o_sg, br, consts, tm):
    t = x.shape[0]
    row = lambda w: pl.BlockSpec((tm, w), lambda i: (i, 0))
    return pl.pallas_call(
        _post_body,
        grid=(t // tm,),
        in_specs=[row(D_MODEL), row(DN_WIDTH), row(DN_WIDTH), row(DN_WIDTH), row(DA_WIDTH), row(SG_WIDTH),
                  row(N_BRANCH * D_MODEL)] + [_const_spec(c.shape) for c in consts],
        out_specs=row(D_MODEL),
        out_shape=jax.ShapeDtypeStruct((t, D_MODEL), F32),
        name="merge_mlp",
        compiler_params=_cparams("parallel"),
    )(x, o_f, o_b, z, o_att, o_sg, br, *consts)


def _np_consts():
    ones_bd = np.kron(np.eye(DN_HEADS), np.ones((DN_HEAD_DIM, DN_HEAD_DIM))).astype(np.float32)
    idx = np.arange(DN_CHUNK)
    tri = np.concatenate([idx[:, None] >= idx[None, :], idx[:, None] <= idx[None, :]], axis=0).astype(np.float32)
    bd4 = np.kron(np.eye(DN_GROUP), np.ones((DN_CHUNK, DN_HEAD_DIM))).astype(np.float32)
    bd8 = np.kron(np.eye(SG_GROUPS), np.ones((SG_CHUNK, SG_GROUP_DIM))).astype(np.float32)
    expand = np.zeros((2, GATE_LANES, 2 * DN_NGROUPS * DN_GROUP_LANES), np.float32)
    for d in range(2):
        for kind in range(2):
            for h in range(DN_HEADS):
                src = kind * 2 * DN_HEADS + d * DN_HEADS + h
                dst = kind * DN_WIDTH + h * DN_HEAD_DIM
                expand[d, src, dst:dst + DN_HEAD_DIM] = 1.0
    as16 = lambda a: jnp.asarray(a, dtype=BF16)
    return as16(ones_bd), as16(tri), as16(bd4), as16(bd8), as16(expand)


def _layer_weights(l, g_mix_pre, w_in, conv_w, dn_A_log, dn_dt_bias, dn_norm_g, da_lambda, da_subln_g,
                   sg_ln_g, sg_ln_b, sg_w, sg_b, w_branch, w_out, g_mix_post, g_mlp_pre, w_up, w_down,
                   g_mlp_post):
    sizes = (DN_WIDTH, DN_WIDTH, DN_WIDTH, DN_WIDTH, 4 * DN_HEADS, DA_WIDTH, DA_WIDTH, DA_WIDTH,
             2 * SG_WIDTH, N_BRANCH * D_MODEL)
    offs = np.concatenate([[0], np.cumsum(sizes)])
    w = w_in[l].astype(BF16)
    col = lambda a, b: w[:, int(offs[a]):int(offs[b])]
    w_gates = jnp.pad(col(4, 5), ((0, 0), (0, GATE_LANES - 4 * DN_HEADS)))
    in_ws = [col(0, 3), col(3, 4), w_gates, col(5, 6), col(6, 7), col(7, 8), col(8, 9), col(9, 10)]
    row = lambda v: v.astype(F32).reshape(1, -1)
    pad_lanes = lambda v: jnp.pad(v.astype(F32).reshape(1, -1), ((0, 0), (0, GATE_LANES - 2 * DN_HEADS)))
    return dict(
        g_pre=row(g_mix_pre[l]), in_ws=in_ws,
        conv_w=conv_w[l].astype(F32), alog=pad_lanes(dn_A_log[l]), dtb=pad_lanes(dn_dt_bias[l]),
        lam=da_lambda[l].astype(F32), subln=row(da_subln_g[l]),
        ln_g=row(sg_ln_g[l]), ln_b=row(sg_ln_b[l]),
        sg_wcat=jnp.transpose(sg_w[l], (1, 0, 2)).reshape(SG_CHUNK, SG_GROUPS * SG_CHUNK).astype(BF16),
        sg_bias=jnp.repeat(sg_b[l].astype(F32).T, SG_GROUP_DIM, axis=1),
        dn_g=jnp.tile(dn_norm_g[l].astype(F32), DN_HEADS)[None, :],
        wb=w_branch[l].astype(BF16), wout=w_out[l].astype(BF16), g_post=row(g_mix_post[l]),
        g_pre2=row(g_mlp_pre[l]), wup=w_up[l].astype(BF16), wdown=w_down[l].astype(BF16),
        g_post2=row(g_mlp_post[l]),
    )


def _lambda_init(layer):
    return 0.8 - 0.6 * math.exp(-0.3 * layer)


def _tile(n, pref):
    return pref if n % pref == 0 else n


def _run_layer(x, lw, lam_init, cos, sin, consts, batch, seq):
    ones_bd, tri, bd4, bd8, expand = consts
    t = batch * seq
    qkv, z, gates, q_rot, k_rot, v_da, sg_uv, br = _inproj(
        x, lw["g_pre"], cos, sin, lw["in_ws"], seq, _tile(seq, 512))
    dq, dk, dv, dgates = _dnprep(qkv, gates, lw["conv_w"], lw["alog"], lw["dtb"], ones_bd, tri,
                                 seq, _tile(seq, 256))
    o_f, o_b = _dn_main(dq, dk, dv, dgates, expand, bd4, batch, seq, _tile(seq // DN_CHUNK, 4))
    o_att = _diff_attn(q_rot, k_rot, v_da, lw["lam"], lw["subln"], batch, seq,
                       _tile(seq, 256), _tile(seq, 512), lam_init)
    o_sg = _sgu(sg_uv, lw["ln_g"], lw["ln_b"], lw["sg_wcat"], lw["sg_bias"], bd8, _tile(t, 512))
    post_consts = (lw["dn_g"], ones_bd, lw["wb"], lw["wout"], lw["g_post"], lw["g_pre2"], lw["wup"],
                   lw["wdown"], lw["g_post2"])
    return _post(x, o_f, o_b, z, o_att, o_sg, br, post_consts, _tile(t, 512))


def kernel(x_prompt, x_sample, g_mix_pre, w_in, conv_w, dn_A_log, dn_dt_bias, dn_norm_g, da_lambda, da_subln_g,
           sg_ln_g, sg_ln_b, sg_w, sg_b, w_branch, w_out, g_mix_post, g_mlp_pre, w_up, w_down, g_mlp_post):
    params = (g_mix_pre, w_in, conv_w, dn_A_log, dn_dt_bias, dn_norm_g, da_lambda, da_subln_g, sg_ln_g, sg_ln_b,
              sg_w, sg_b, w_branch, w_out, g_mix_post, g_mlp_pre, w_up, w_down, g_mlp_post)
    depth = w_in.shape[0]
    consts = _np_consts()
    layers = [_layer_weights(l, *params) for l in range(depth)]
    cos, sin = _rope_tables(max(x_prompt.shape[1], x_sample.shape[1]))

    def trunk(x):
        batch, seq, _ = x.shape
        h = x.astype(F32).reshape(batch * seq, D_MODEL)
        for l in range(depth):
            h = _run_layer(h, layers[l], _lambda_init(l), cos, sin, consts, batch, seq)
        return h.reshape(batch, seq, D_MODEL)

    return trunk(x_prompt), trunk(x_sample)
```

```python
import functools
import math

import numpy as np
import jax
import jax.numpy as jnp
from jax import lax
from jax.experimental import pallas as pl
from jax.experimental.pallas import tpu as pltpu

F32 = jnp.float32
BF16 = jnp.bfloat16

D_MODEL = 1024
DN_HEADS = 8
DN_HEAD_DIM = 64
DN_WIDTH = DN_HEADS * DN_HEAD_DIM
DN_CONV = 5
DN_CHUNK = 64
DA_HEADS = 4
DA_HEAD_DIM = 64
DA_WIDTH = DA_HEADS * 2 * DA_HEAD_DIM
ROPE_THETA = 10000.0
SG_GROUPS = 8
SG_WIDTH = 512
SG_GROUP_DIM = SG_WIDTH // SG_GROUPS
SG_CHUNK = 128
N_BRANCH = 3
NORM_EPS = 1e-6
L2_EPS = 1e-6

LANES = 128
GATE_LANES = LANES
DN_GROUP = 4
DN_GROUP_LANES = DN_GROUP * DN_HEAD_DIM
DN_NGROUPS = DN_HEADS // DN_GROUP
HALO_ROWS = 16
DA_SOFTMAX_ROWS = 32

VMEM_LIMIT = 56 * 2**20


def _cparams(*sem):
    return pltpu.CompilerParams(dimension_semantics=sem, vmem_limit_bytes=VMEM_LIMIT)


def _const_spec(shape):
    nd = len(shape)
    return pl.BlockSpec(shape, lambda *_: (0,) * nd, pipeline_mode=pl.Buffered(1))


def _dot(a, b):
    return jnp.dot(a, b, preferred_element_type=F32)


def _dot_nt(a, b):
    return lax.dot_general(a, b, (((1,), (1,)), ((), ())), preferred_element_type=F32)


def _split2(x):
    hi = x.astype(BF16)
    lo = (x - hi.astype(F32)).astype(BF16)
    return hi, lo


def _rms(x, g):
    return x * lax.rsqrt(jnp.mean(x * x, axis=-1, keepdims=True) + NORM_EPS) * g


def _rope_body(inv_ref, cos_ref, sin_ref):
    tb = cos_ref.shape[0]
    row = lax.broadcasted_iota(jnp.int32, (tb, LANES), 0) + pl.program_id(0) * tb
    lane = lax.broadcasted_iota(jnp.int32, (tb, LANES), 1)
    ang = row.astype(F32) * inv_ref[...]
    half = DA_HEAD_DIM // 2
    sign = jnp.where(lane % DA_HEAD_DIM < half, -1.0, 1.0)
    cos_ref[...] = jnp.cos(ang)
    sin_ref[...] = jnp.sin(ang) * sign


def _rope_tables(seq):
    half = DA_HEAD_DIM // 2
    inv = jnp.power(ROPE_THETA, -jnp.arange(half, dtype=F32) / half)
    inv = jnp.tile(inv, LANES // half)[None, :]
    tb = min(seq, 1024)
    return pl.pallas_call(
        _rope_body,
        grid=(seq // tb,),
        in_specs=[pl.BlockSpec((1, LANES), lambda i: (0, 0))],
        out_specs=[pl.BlockSpec((tb, LANES), lambda i: (i, 0))] * 2,
        out_shape=[jax.ShapeDtypeStruct((seq, LANES), F32)] * 2,
        name="rope_tables",
        compiler_params=_cparams("parallel"),
    )(inv)


def _inproj_body(x_ref, g_ref, cos_ref, sin_ref, wqkv, wz, wg, wq, wk, wv, wsg, wbr,
                 oqkv, oz, og, oq, ok, ov, osg, obr, h_scr):
    h_scr[...] = _rms(x_ref[...], g_ref[...]).astype(BF16)
    tm = x_ref.shape[0]
    chunk = 512

    def proj(w_ref, o_ref, epilogue=None):
        width = w_ref.shape[1]
        step = min(chunk, width)
        for c0 in range(0, width, step):
            y = _dot(h_scr[...], w_ref[:, c0:c0 + step])
            if epilogue is not None:
                y = epilogue(y)
            o_ref[:, c0:c0 + step] = y.astype(o_ref.dtype)

    reps = DA_WIDTH // LANES
    cos = jnp.concatenate([cos_ref[...]] * reps, axis=1)
    sin = jnp.concatenate([sin_ref[...]] * reps, axis=1)
    lane = lax.broadcasted_iota(jnp.int32, (tm, DA_WIDTH), 1)
    first_half = lane % DA_HEAD_DIM < DA_HEAD_DIM // 2
    half = DA_HEAD_DIM // 2

    def rope(scale):
        def f(y):
            up = pltpu.roll(y, DA_WIDTH - half, axis=1)
            dn = pltpu.roll(y, half, axis=1)
            out = y * cos + jnp.where(first_half, up, dn) * sin
            return out * scale if scale != 1.0 else out
        return f

    proj(wqkv, oqkv)
    proj(wz, oz)
    proj(wg, og)
    proj(wq, oq, rope(DA_HEAD_DIM ** -0.5 * math.log2(math.e)))
    proj(wk, ok, rope(1.0))
    proj(wv, ov)
    proj(wsg, osg)
    proj(wbr, obr)


def _inproj(x, g, cos, sin, ws, seq, tm):
    t = x.shape[0]
    row = lambda w: pl.BlockSpec((tm, w), lambda i: (i, 0))
    tiles_per_seq = seq // tm
    tab = pl.BlockSpec((tm, LANES), lambda i: (i % tiles_per_seq, 0))
    widths = [w.shape[1] for w in ws]
    dtypes = [BF16, BF16, F32, BF16, BF16, BF16, BF16, BF16]
    return pl.pallas_call(
        _inproj_body,
        grid=(t // tm,),
        in_specs=[row(D_MODEL), _const_spec((1, D_MODEL)), tab, tab] + [_const_spec(w.shape) for w in ws],
        out_specs=[row(w) for w in widths],
        out_shape=[jax.ShapeDtypeStruct((t, w), d) for w, d in zip(widths, dtypes)],
        scratch_shapes=[pltpu.VMEM((tm, D_MODEL), BF16)],
        name="in_proj",
        compiler_params=_cparams("parallel"),
    )(x, g, cos, sin, *ws)


def _dnprep_body(x_ref, xp_ref, xn_ref, gts_ref, cw_ref, alog_ref, dtb_ref, ones_ref, tri_ref,
                 q_ref, k_ref, v_ref, go_ref, ext_scr, *, tiles_per_seq):
    i = pl.program_id(0)
    tm = x_ref.shape[0]
    pos = i % tiles_per_seq
    half = HALO_ROWS // 2
    keep_prev = (pos != 0).astype(F32)
    keep_next = (pos != tiles_per_seq - 1).astype(F32)
    ext_scr[0:half, :] = xp_ref[...].astype(F32)[half:, :] * keep_prev
    ext_scr[half:half + tm, :] = x_ref[...].astype(F32)
    ext_scr[half + tm:HALO_ROWS + tm, :] = xn_ref[...].astype(F32)[:half, :] * keep_next

    ones = ones_ref[...]
    pad = DN_CONV // 2
    for part, o_ref in enumerate((q_ref, k_ref, v_ref)):
        ls = slice(DN_WIDTH * part, DN_WIDTH * (part + 1))
        acc = None
        for j in range(DN_CONV):
            off = half - pad + j
            term = ext_scr[off:off + tm, ls] * cw_ref[j:j + 1, ls]
            acc = term if acc is None else acc + term
        y = acc * jax.nn.sigmoid(acc)
        if part < 2:
            hi, lo = _split2(y * y)
            ss = _dot(hi, ones) + _dot(lo, ones)
            y = y * lax.rsqrt(ss + L2_EPS)
            if part == 0:
                y = y * (DN_HEAD_DIM ** -0.5)
        o_ref[...] = y.astype(o_ref.dtype)

    gl = gts_ref[...]
    lane = lax.broadcasted_iota(jnp.int32, (DN_CHUNK, GATE_LANES), 1)
    z = gl + dtb_ref[...]
    softplus = jnp.maximum(z, 0.0) + jnp.log1p(jnp.exp(-jnp.abs(z)))
    g = -jnp.exp(alog_ref[...]) * softplus
    beta = jax.nn.sigmoid(gl)
    tri = tri_ref[...]
    for c in range(tm // DN_CHUNK):
        rows = slice(c * DN_CHUNK, (c + 1) * DN_CHUNK)
        gs = g[rows]
        hi = gs.astype(BF16)
        r1 = gs - hi.astype(F32)
        mid = r1.astype(BF16)
        lo = (r1 - mid.astype(F32)).astype(BF16)
        cs = _dot(tri, jnp.concatenate([hi, mid, lo], axis=1))
        cs = cs[:, :GATE_LANES] + cs[:, GATE_LANES:2 * GATE_LANES] + cs[:, 2 * GATE_LANES:]
        fwd, bwd = cs[:DN_CHUNK], cs[DN_CHUNK:]
        go_ref[rows, :] = jnp.where(lane < DN_HEADS, fwd, jnp.where(lane < 2 * DN_HEADS, bwd, beta[rows]))


def _dnprep(qkv, gates, conv_w, alog, dtb, ones_bd, tri, seq, tm):
    t = qkv.shape[0]
    cw = qkv.shape[1]
    hb = tm // HALO_ROWS
    nhalo = t // HALO_ROWS
    tiles_per_seq = seq // tm
    out = pl.BlockSpec((tm, DN_WIDTH), lambda i: (i, 0))
    return pl.pallas_call(
        functools.partial(_dnprep_body, tiles_per_seq=tiles_per_seq),
        grid=(t // tm,),
        in_specs=[
            pl.BlockSpec((tm, cw), lambda i: (i, 0)),
            pl.BlockSpec((HALO_ROWS, cw), lambda i: (jnp.maximum(i * hb - 1, 0), 0)),
            pl.BlockSpec((HALO_ROWS, cw), lambda i: (jnp.minimum((i + 1) * hb, nhalo - 1), 0)),
            pl.BlockSpec((tm, GATE_LANES), lambda i: (i, 0)),
            _const_spec(conv_w.shape), _const_spec(alog.shape), _const_spec(dtb.shape),
            _const_spec(ones_bd.shape), _const_spec(tri.shape),
        ],
        out_specs=[out, out, out, pl.BlockSpec((tm, GATE_LANES), lambda i: (i, 0))],
        out_shape=[jax.ShapeDtypeStruct((t, DN_WIDTH), BF16)] * 3 + [jax.ShapeDtypeStruct((t, GATE_LANES), F32)],
        scratch_shapes=[pltpu.VMEM((tm + HALO_ROWS, cw), F32)],
        name="dn_prep",
        compiler_params=_cparams("parallel"),
    )(qkv, qkv, qkv, gates, conv_w, alog, dtb, ones_bd, tri)


def _dn_chunks(streams, bd, eye):
    def blockdiag(x16):
        return jnp.concatenate([x16] * DN_GROUP, axis=0) * bd

    n = range(len(streams))
    q, k, v, gc, beta, s_refs, strict, last_row, offdiag = (list(col) for col in zip(*streams))
    kf = [k[i].astype(F32) for i in n]
    kb = [kf[i] * beta[i] for i in n]
    pair = [_dot_nt(jnp.concatenate([kb[i].astype(BF16), q[i]], axis=0), blockdiag(k[i])) for i in n]
    gc_row = [jnp.sum(gc[i] * eye, axis=0, keepdims=True) for i in n]
    dec = [jnp.exp(jnp.where(strict[i] > 0.5, gc[i] - gc_row[i], -jnp.inf)) for i in n]
    qk = [pair[i][DN_CHUNK:] * (dec[i] + eye) for i in n]

    gm = [pair[i][:DN_CHUNK] * dec[i] for i in n]
    for lvl in range(int(math.log2(DN_CHUNK))):
        g16 = [gm[i].astype(BF16) for i in n]
        gm = [gm[i] - _dot(g16[i] * offdiag[i][lvl], blockdiag(g16[i])) for i in n]

    egc = [jnp.exp(gc[i]) for i in n]
    vb = [v[i].astype(F32) * beta[i] for i in n]
    kbg = [kb[i] * egc[i] for i in n]
    uw = [jnp.concatenate([vb[i], kbg[i]], axis=1)
          - _dot(gm[i].astype(BF16), jnp.concatenate([blockdiag(vb[i].astype(BF16)),
                                                      blockdiag(kbg[i].astype(BF16))], axis=1)) for i in n]

    state = [s_refs[i][...] for i in n]
    qg = [q[i].astype(F32) * egc[i] for i in n]
    ws = [_dot(jnp.concatenate([uw[i][:, DN_GROUP_LANES:].astype(BF16), qg[i].astype(BF16)], axis=0),
               state[i].astype(BF16)) for i in n]
    v_new = [(uw[i][:, :DN_GROUP_LANES] - ws[i][:DN_CHUNK]).astype(BF16) for i in n]
    o = [ws[i][DN_CHUNK:] + _dot(qk[i].astype(BF16), blockdiag(v_new[i])) for i in n]
    g_last = [gc[i][last_row[i]:last_row[i] + 1, :] for i in n]
    kdec = [kf[i] * jnp.exp(g_last[i] - gc[i]) for i in n]
    upd = [_dot(kdec[i].T.astype(BF16), v_new[i]) for i in n]
    bdf = bd.astype(F32)
    for i in n:
        s_refs[i][...] = (state[i] * jnp.exp(g_last[i]) + upd[i]) * bdf
    return o


def _dn_body(qf, kf, vf, gf, qb, kb, vb, gb, exp_ref, bd_ref, od_ref, of_ref, ob_ref, state, gexp, *, cb):
    i = pl.program_id(1)

    @pl.when(i == 0)
    def _():
        state[...] = jnp.zeros_like(state)

    for d, g_ref in enumerate((gf, gb)):
        hi, lo = _split2(g_ref[...])
        e = _dot(jnp.concatenate([hi, lo], axis=0), exp_ref[d])
        rows = g_ref.shape[0]
        gexp[d] = e[:rows] + e[rows:]

    bd = bd_ref[...]
    ri = lax.broadcasted_iota(jnp.int32, (DN_CHUNK, DN_GROUP_LANES), 0)
    cj = lax.broadcasted_iota(jnp.int32, (DN_CHUNK, DN_GROUP_LANES), 1) % DN_CHUNK
    eye = (ri == cj).astype(F32)
    lower = (ri > cj).astype(F32)
    upper = (ri < cj).astype(F32)
    gl = DN_GROUP_LANES

    def body(c, carry):
        streams, dests = [], []
        for d, (q_ref, k_ref, v_ref, o_ref) in enumerate(((qf, kf, vf, of_ref), (qb, kb, vb, ob_ref))):
            cc = c if d == 0 else cb - 1 - c
            rows = pl.ds(pl.multiple_of(cc * DN_CHUNK, DN_CHUNK), DN_CHUNK)
            for grp in range(DN_NGROUPS):
                ls = slice(grp * gl, (grp + 1) * gl)
                gc = gexp[d, rows, grp * gl:(grp + 1) * gl]
                beta = gexp[d, rows, (DN_NGROUPS + grp) * gl:(DN_NGROUPS + grp + 1) * gl]
                streams.append((q_ref[rows, ls], k_ref[rows, ls], v_ref[rows, ls], gc, beta,
                                state.at[d * DN_NGROUPS + grp],
                                lower if d == 0 else upper, DN_CHUNK - 1 if d == 0 else 0, od_ref.at[d]))
                dests.append((o_ref, rows, ls))
        for o, (o_ref, rows, ls) in zip(_dn_chunks(streams, bd, eye), dests):
            o_ref[rows, ls] = o.astype(o_ref.dtype)
        return carry

    lax.fori_loop(0, cb, body, 0)


def _dn_main(q, k, v, gates, expand, bd, offdiag, batch, seq, cb):
    t = q.shape[0]
    rows = cb * DN_CHUNK
    nb = seq // rows
    fwd = lambda w: pl.BlockSpec((rows, w), lambda b, i: (b * nb + i, 0))
    bwd = lambda w: pl.BlockSpec((rows, w), lambda b, i: (b * nb + nb - 1 - i, 0))
    return pl.pallas_call(
        functools.partial(_dn_body, cb=cb),
        grid=(batch, nb),
        in_specs=[fwd(DN_WIDTH)] * 3 + [fwd(GATE_LANES)] + [bwd(DN_WIDTH)] * 3 + [bwd(GATE_LANES)]
                 + [_const_spec(expand.shape), _const_spec(bd.shape), _const_spec(offdiag.shape)],
        out_specs=[fwd(DN_WIDTH), bwd(DN_WIDTH)],
        out_shape=[jax.ShapeDtypeStruct((t, DN_WIDTH), BF16)] * 2,
        scratch_shapes=[pltpu.VMEM((2 * DN_NGROUPS, DN_GROUP_LANES, DN_GROUP_LANES), F32),
                        pltpu.VMEM((2, rows, 2 * DN_NGROUPS * DN_GROUP_LANES), F32)],
        name="dn_delta_rule",
        compiler_params=_cparams("arbitrary", "arbitrary"),
    )(q, k, v, gates, q, k, v, gates, expand, bd, offdiag)


def _da_body(q_ref, k_ref, v_ref, lam_ref, g_ref, o_ref, q2_scr, s_scr, p_scr, m_scr, a_scr, acc_scr,
             *, tk, lam_init):
    tq = q_ref.shape[0]
    seq = k_ref.shape[0]
    lane = lax.broadcasted_iota(jnp.int32, (tq, LANES), 1)
    q = q_ref[...]
    zero = jnp.zeros_like(q)
    q2_scr[0:tq, :] = jnp.where(lane < DA_HEAD_DIM, q, zero)
    q2_scr[tq:2 * tq, :] = jnp.where(lane < DA_HEAD_DIM, zero, q)

    nk = seq // tk
    assert nk % 2 == 0
    ones = jnp.ones((tk, LANES), BF16)

    def key_rows(j):
        return pl.ds(pl.multiple_of(j * tk, tk), tk)

    def qk_stage(j, slot):
        s_scr[slot] = _dot_nt(q2_scr[...], k_ref[key_rows(j), :])

    def softmax_stage(slot):
        for r0 in range(0, 2 * tq, DA_SOFTMAX_ROWS):
            rows = slice(r0, r0 + DA_SOFTMAX_ROWS)
            s = s_scr[slot, rows, :]
            m_old = m_scr[rows, :]
            m_new = jnp.maximum(m_old, jnp.max(s, axis=1, keepdims=True))
            p_scr[slot, rows, :] = jnp.exp2(s - m_new).astype(BF16)
            a_scr[slot, rows, :] = jnp.exp2(m_old - m_new)
            m_scr[rows, :] = m_new

    def pv_stage(j, slot):
        v_aug = jnp.concatenate([v_ref[key_rows(j), :], ones], axis=1)
        acc_scr[...] = a_scr[slot] * acc_scr[...] + _dot(p_scr[slot], v_aug)

    m_scr[...] = jnp.full(m_scr.shape, -jnp.inf, F32)
    acc_scr[...] = jnp.zeros(acc_scr.shape, F32)
    qk_stage(0, 0)
    qk_stage(1, 1)
    softmax_stage(0)

    def step(i, carry):
        j = 2 * i
        qk_stage(j + 2, 0)
        softmax_stage(1)
        pv_stage(j, 0)
        qk_stage(j + 3, 1)
        softmax_stage(0)
        pv_stage(j + 1, 1)
        return carry

    lax.fori_loop(0, nk // 2 - 1, step, 0)
    softmax_stage(1)
    pv_stage(nk - 2, 0)
    pv_stage(nk - 1, 1)
    o = acc_scr[:, :LANES] / acc_scr[:, LANES:]
    lp = lam_ref[...]
    lam = (jnp.exp(jnp.sum(lp[0:1] * lp[1:2], axis=1, keepdims=True))
           - jnp.exp(jnp.sum(lp[2:3] * lp[3:4], axis=1, keepdims=True)) + lam_init)
    od = o[:tq] - lam * o[tq:]
    o_ref[...] = (_rms(od, g_ref[...]) * (1.0 - lam_init)).astype(o_ref.dtype)


def _diff_attn(q, k, v, lam, g, batch, seq, tq, tk, lam_init):
    t = q.shape[0]
    nq = seq // tq
    qspec = pl.BlockSpec((tq, LANES), lambda b, h, i: (b * nq + i, h))
    kspec = pl.BlockSpec((seq, LANES), lambda b, h, i: (b, h))
    return pl.pallas_call(
        functools.partial(_da_body, tk=tk, lam_init=lam_init),
        grid=(batch, DA_HEADS, nq),
        in_specs=[qspec, kspec, kspec,
                  pl.BlockSpec(lam.shape, lambda b, h, i: (0, 0)),
                  pl.BlockSpec(g.shape, lambda b, h, i: (0, 0))],
        out_specs=qspec,
        out_shape=jax.ShapeDtypeStruct((t, DA_WIDTH), BF16),
        scratch_shapes=[pltpu.VMEM((2 * tq, LANES), BF16),
                        pltpu.VMEM((2, 2 * tq, tk), F32),
                        pltpu.VMEM((2, 2 * tq, tk), BF16),
                        pltpu.VMEM((2 * tq, 1), F32),
                        pltpu.VMEM((2, 2 * tq, 1), F32),
                        pltpu.VMEM((2 * tq, 2 * LANES), F32)],
        name="diff_attn",
        compiler_params=_cparams("parallel", "parallel", "arbitrary"),
    )(q, k, v, lam, g)


def _gelu_tanh(x):
    return 0.5 * x * (1.0 + jnp.tanh(math.sqrt(2.0 / math.pi) * (x + 0.044715 * (x * x * x))))


def _sgu_body(uv_ref, lng_ref, lnb_ref, w_ref, bias_ref, bd_ref, o_ref):
    tm = uv_ref.shape[0]
    w = w_ref[...]
    bd = bd_ref[...]
    for c in range(tm // SG_CHUNK):
        rows = slice(c * SG_CHUNK, (c + 1) * SG_CHUNK)
        uv = _gelu_tanh(uv_ref[rows, :].astype(F32))
        u, v = uv[:, :SG_WIDTH], uv[:, SG_WIDTH:]
        mu = jnp.mean(v, axis=-1, keepdims=True)
        vc = v - mu
        vn = vc * lax.rsqrt(jnp.mean(vc * vc, axis=-1, keepdims=True) + NORM_EPS) * lng_ref[...] + lnb_ref[...]
        vbd = jnp.concatenate([vn.astype(BF16)] * SG_GROUPS, axis=0) * bd
        mixed = _dot(w, vbd) + bias_ref[...]
        o_ref[rows, :] = (u * mixed).astype(o_ref.dtype)


def _sgu(uv, ln_g, ln_b, wcat, bias, bd, tm):
    t = uv.shape[0]
    return pl.pallas_call(
        _sgu_body,
        grid=(t // tm,),
        in_specs=[pl.BlockSpec((tm, 2 * SG_WIDTH), lambda i: (i, 0)),
                  _const_spec(ln_g.shape), _const_spec(ln_b.shape), _const_spec(wcat.shape),
                  _const_spec(bias.shape), _const_spec(bd.shape)],
        out_specs=pl.BlockSpec((tm, SG_WIDTH), lambda i: (i, 0)),
        out_shape=jax.ShapeDtypeStruct((t, SG_WIDTH), BF16),
        name="spatial_gating",
        compiler_params=_cparams("parallel"),
    )(uv, ln_g, ln_b, wcat, bias, bd)


def _post_body(x_ref, of_ref, ob_ref, z_ref, oatt_ref, osg_ref, br_ref, dng_ref, ones_ref,
               wb_ref, wout_ref, gpost_ref, gpre2_ref, wup_ref, wdown_ref, gpost2_ref, y_ref):
    od = of_ref[...].astype(F32) + ob_ref[...].astype(F32)
    hi, lo = _split2(od * od)
    ones = ones_ref[...]
    ms = (_dot(hi, ones) + _dot(lo, ones)) * (1.0 / DN_HEAD_DIM)
    z = z_ref[...].astype(F32)
    o_a = od * lax.rsqrt(ms + NORM_EPS) * dng_ref[...] * (z * jax.nn.sigmoid(z))

    merged = None
    for idx, o_i in enumerate((o_a.astype(BF16), oatt_ref[...], osg_ref[...])):
        gate = jax.nn.sigmoid(br_ref[:, idx * D_MODEL:(idx + 1) * D_MODEL].astype(F32))
        term = gate * _dot(o_i, wb_ref[idx])
        merged = term if merged is None else merged + term
    mix = _dot(merged.astype(BF16), wout_ref[...])
    x1 = x_ref[...] + _rms(mix, gpost_ref[...])

    hm = _rms(x1, gpre2_ref[...]).astype(BF16)
    d_ff = wup_ref.shape[1]
    chunk = 1024
    f = None
    for c0 in range(0, d_ff, chunk):
        hid = jnp.maximum(_dot(hm, wup_ref[:, c0:c0 + chunk]), 0.0)
        term = _dot((hid * hid).astype(BF16), wdown_ref[c0:c0 + chunk, :])
        f = term if f is None else f + term
    y_ref[...] = x1 + _rms(f, gpost2_ref[...])


def _post(x, o_f, o_b, z, o_att, o_sg, br, consts, tm):
    t = x.shape[0]
    row = lambda w: pl.BlockSpec((tm, w), lambda i: (i, 0))
    return pl.pallas_call(
        _post_body,
        grid=(t // tm,),
        in_specs=[row(D_MODEL), row(DN_WIDTH), row(DN_WIDTH), row(DN_WIDTH), row(DA_WIDTH), row(SG_WIDTH),
                  row(N_BRANCH * D_MODEL)] + [_const_spec(c.shape) for c in consts],
        out_specs=row(D_MODEL),
        out_shape=jax.ShapeDtypeStruct((t, D_MODEL), F32),
        name="merge_mlp",
        compiler_params=_cparams("parallel"),
    )(x, o_f, o_b, z, o_att, o_sg, br, *consts)


def _np_consts():
    ones_bd = np.kron(np.eye(DN_HEADS), np.ones((DN_HEAD_DIM, DN_HEAD_DIM))).astype(np.float32)
    idx = np.arange(DN_CHUNK)
    tri = np.concatenate([idx[:, None] >= idx[None, :], idx[:, None] <= idx[None, :]], axis=0).astype(np.float32)
    bd4 = np.kron(np.eye(DN_GROUP), np.ones((DN_CHUNK, DN_HEAD_DIM))).astype(np.float32)
    bd8 = np.kron(np.eye(SG_GROUPS), np.ones((SG_CHUNK, SG_GROUP_DIM))).astype(np.float32)
    expand = np.zeros((2, GATE_LANES, 2 * DN_NGROUPS * DN_GROUP_LANES), np.float32)
    for d in range(2):
        for kind in range(2):
            for h in range(DN_HEADS):
                src = kind * 2 * DN_HEADS + d * DN_HEADS + h
                dst = kind * DN_WIDTH + h * DN_HEAD_DIM
                expand[d, src, dst:dst + DN_HEAD_DIM] = 1.0
    levels = int(math.log2(DN_CHUNK))
    offdiag = np.zeros((2, levels, DN_CHUNK, DN_CHUNK), np.float32)
    for lvl in range(levels):
        s = 1 << lvl
        same = (idx[:, None] // (2 * s)) == (idx[None, :] // (2 * s))
        hi_half, lo_half = (idx // s) % 2 == 1, (idx // s) % 2 == 0
        offdiag[0, lvl] = same & hi_half[:, None] & lo_half[None, :]
        offdiag[1, lvl] = same & lo_half[:, None] & hi_half[None, :]
    offdiag = np.tile(offdiag, (1, 1, 1, DN_GROUP))
    as16 = lambda a: jnp.asarray(a, dtype=BF16)
    return as16(ones_bd), as16(tri), as16(bd4), as16(bd8), as16(expand), as16(offdiag)


def _layer_weights(l, g_mix_pre, w_in, conv_w, dn_A_log, dn_dt_bias, dn_norm_g, da_lambda, da_subln_g,
                   sg_ln_g, sg_ln_b, sg_w, sg_b, w_branch, w_out, g_mix_post, g_mlp_pre, w_up, w_down,
                   g_mlp_post):
    sizes = (DN_WIDTH, DN_WIDTH, DN_WIDTH, DN_WIDTH, 4 * DN_HEADS, DA_WIDTH, DA_WIDTH, DA_WIDTH,
             2 * SG_WIDTH, N_BRANCH * D_MODEL)
    offs = np.concatenate([[0], np.cumsum(sizes)])
    w = w_in[l].astype(BF16)
    col = lambda a, b: w[:, int(offs[a]):int(offs[b])]
    w_gates = jnp.pad(col(4, 5), ((0, 0), (0, GATE_LANES - 4 * DN_HEADS)))
    in_ws = [col(0, 3), col(3, 4), w_gates, col(5, 6), col(6, 7), col(7, 8), col(8, 9), col(9, 10)]
    row = lambda v: v.astype(F32).reshape(1, -1)
    pad_lanes = lambda v: jnp.pad(v.astype(F32).reshape(1, -1), ((0, 0), (0, GATE_LANES - 2 * DN_HEADS)))
    return dict(
        g_pre=row(g_mix_pre[l]), in_ws=in_ws,
        conv_w=conv_w[l].astype(F32), alog=pad_lanes(dn_A_log[l]), dtb=pad_lanes(dn_dt_bias[l]),
        lam=da_lambda[l].astype(F32), subln=row(da_subln_g[l]),
        ln_g=row(sg_ln_g[l]), ln_b=row(sg_ln_b[l]),
        sg_wcat=jnp.transpose(sg_w[l], (1, 0, 2)).reshape(SG_CHUNK, SG_GROUPS * SG_CHUNK).astype(BF16),
        sg_bias=jnp.repeat(sg_b[l].astype(F32).T, SG_GROUP_DIM, axis=1),
        dn_g=jnp.tile(dn_norm_g[l].astype(F32), DN_HEADS)[None, :],
        wb=w_branch[l].astype(BF16), wout=w_out[l].astype(BF16), g_post=row(g_mix_post[l]),
        g_pre2=row(g_mlp_pre[l]), wup=w_up[l].astype(BF16), wdown=w_down[l].astype(BF16),
        g_post2=row(g_mlp_post[l]),
    )


def _lambda_init(layer):
    return 0.8 - 0.6 * math.exp(-0.3 * layer)


def _tile(n, pref):
    return pref if n % pref == 0 else n


def _run_layer(x, lw, lam_init, cos, sin, consts, batch, seq):
    ones_bd, tri, bd4, bd8, expand, offdiag = consts
    t = batch * seq
    qkv, z, gates, q_rot, k_rot, v_da, sg_uv, br = _inproj(
        x, lw["g_pre"], cos, sin, lw["in_ws"], seq, _tile(seq, 512))
    dq, dk, dv, dgates = _dnprep(qkv, gates, lw["conv_w"], lw["alog"], lw["dtb"], ones_bd, tri,
                                 seq, _tile(seq, 256))
    o_f, o_b = _dn_main(dq, dk, dv, dgates, expand, bd4, offdiag, batch, seq, _tile(seq // DN_CHUNK, 8))
    o_att = _diff_attn(q_rot, k_rot, v_da, lw["lam"], lw["subln"], batch, seq,
                       _tile(seq, 256), min(1024, seq // 2), lam_init)
    o_sg = _sgu(sg_uv, lw["ln_g"], lw["ln_b"], lw["sg_wcat"], lw["sg_bias"], bd8, _tile(t, 512))
    post_consts = (lw["dn_g"], ones_bd, lw["wb"], lw["wout"], lw["g_post"], lw["g_pre2"], lw["wup"],
                   lw["wdown"], lw["g_post2"])
    return _post(x, o_f, o_b, z, o_att, o_sg, br, post_consts, _tile(t, 512))


def kernel(x_prompt, x_sample, g_mix_pre, w_in, conv_w, dn_A_log, dn_dt_bias, dn_norm_g, da_lambda, da_subln_g,
           sg_ln_g, sg_ln_b, sg_w, sg_b, w_branch, w_out, g_mix_post, g_mlp_pre, w_up, w_down, g_mlp_post):
    params = (g_mix_pre, w_in, conv_w, dn_A_log, dn_dt_bias, dn_norm_g, da_lambda, da_subln_g, sg_ln_g, sg_ln_b,
              sg_w, sg_b, w_branch, w_out, g_mix_post, g_mlp_pre, w_up, w_down, g_mlp_post)
    depth = w_in.shape[0]
    consts = _np_consts()
    layers = [_layer_weights(l, *params) for l in range(depth)]
    cos, sin = _rope_tables(max(x_prompt.shape[1], x_sample.shape[1]))

    def trunk(x):
        batch, seq, _ = x.shape
        h = x.astype(F32).reshape(batch * seq, D_MODEL)
        for l in range(depth):
            h = _run_layer(h, layers[l], _lambda_init(l), cos, sin, consts, batch, seq)
        return h.reshape(batch, seq, D_MODEL)

    return trunk(x_prompt), trunk(x_sample)
```

```python
import functools
import math

import numpy as np
import jax
import jax.numpy as jnp
from jax import lax
from jax.experimental import pallas as pl
from jax.experimental.pallas import tpu as pltpu

F32 = jnp.float32
BF16 = jnp.bfloat16

D_MODEL = 1024
DN_HEADS = 8
DN_HEAD_DIM = 64
DN_WIDTH = DN_HEADS * DN_HEAD_DIM
DN_CONV = 5
DN_CHUNK = 64
DA_HEADS = 4
DA_HEAD_DIM = 64
DA_WIDTH = DA_HEADS * 2 * DA_HEAD_DIM
ROPE_THETA = 10000.0
SG_GROUPS = 8
SG_WIDTH = 512
SG_GROUP_DIM = SG_WIDTH // SG_GROUPS
SG_CHUNK = 128
N_BRANCH = 3
NORM_EPS = 1e-6
L2_EPS = 1e-6

LANES = 128
GATE_LANES = LANES
DN_GROUP = 4
DN_GROUP_LANES = DN_GROUP * DN_HEAD_DIM
DN_NGROUPS = DN_HEADS // DN_GROUP
HALO_ROWS = 16
DA_SOFTMAX_ROWS = 32
DN_UNROLL = 2

VMEM_LIMIT = 56 * 2**20


def _cparams(*sem):
    return pltpu.CompilerParams(dimension_semantics=sem, vmem_limit_bytes=VMEM_LIMIT)


def _const_spec(shape):
    nd = len(shape)
    return pl.BlockSpec(shape, lambda *_: (0,) * nd, pipeline_mode=pl.Buffered(1))


def _dot(a, b):
    return jnp.dot(a, b, preferred_element_type=F32)


def _dot_nt(a, b):
    return lax.dot_general(a, b, (((1,), (1,)), ((), ())), preferred_element_type=F32)


def _split2(x):
    hi = x.astype(BF16)
    lo = (x - hi.astype(F32)).astype(BF16)
    return hi, lo


def _rms(x, g):
    return x * lax.rsqrt(jnp.mean(x * x, axis=-1, keepdims=True) + NORM_EPS) * g


def _rope_body(inv_ref, cos_ref, sin_ref):
    tb = cos_ref.shape[0]
    row = lax.broadcasted_iota(jnp.int32, (tb, LANES), 0) + pl.program_id(0) * tb
    lane = lax.broadcasted_iota(jnp.int32, (tb, LANES), 1)
    ang = row.astype(F32) * inv_ref[...]
    half = DA_HEAD_DIM // 2
    sign = jnp.where(lane % DA_HEAD_DIM < half, -1.0, 1.0)
    cos_ref[...] = jnp.cos(ang)
    sin_ref[...] = jnp.sin(ang) * sign


def _rope_tables(seq):
    half = DA_HEAD_DIM // 2
    inv = jnp.power(ROPE_THETA, -jnp.arange(half, dtype=F32) / half)
    inv = jnp.tile(inv, LANES // half)[None, :]
    tb = min(seq, 1024)
    return pl.pallas_call(
        _rope_body,
        grid=(seq // tb,),
        in_specs=[pl.BlockSpec((1, LANES), lambda i: (0, 0))],
        out_specs=[pl.BlockSpec((tb, LANES), lambda i: (i, 0))] * 2,
        out_shape=[jax.ShapeDtypeStruct((seq, LANES), F32)] * 2,
        name="rope_tables",
        compiler_params=_cparams("parallel"),
    )(inv)


def _inproj_body(x_ref, g_ref, cos_ref, sin_ref, wqkv, wz, wg, wq, wk, wv, wsg, wbr,
                 oqkv, oz, og, oq, ok, ov, osg, obr, h_scr):
    h_scr[...] = _rms(x_ref[...], g_ref[...]).astype(BF16)
    tm = x_ref.shape[0]
    chunk = 512

    def proj(w_ref, o_ref, epilogue=None):
        width = w_ref.shape[1]
        step = min(chunk, width)
        for c0 in range(0, width, step):
            y = _dot(h_scr[...], w_ref[:, c0:c0 + step])
            if epilogue is not None:
                y = epilogue(y)
            o_ref[:, c0:c0 + step] = y.astype(o_ref.dtype)

    reps = DA_WIDTH // LANES
    cos = jnp.concatenate([cos_ref[...]] * reps, axis=1)
    sin = jnp.concatenate([sin_ref[...]] * reps, axis=1)
    lane = lax.broadcasted_iota(jnp.int32, (tm, DA_WIDTH), 1)
    first_half = lane % DA_HEAD_DIM < DA_HEAD_DIM // 2
    half = DA_HEAD_DIM // 2

    def rope(scale):
        def f(y):
            up = pltpu.roll(y, DA_WIDTH - half, axis=1)
            dn = pltpu.roll(y, half, axis=1)
            out = y * cos + jnp.where(first_half, up, dn) * sin
            return out * scale if scale != 1.0 else out
        return f

    proj(wqkv, oqkv)
    proj(wz, oz)
    proj(wg, og)
    proj(wq, oq, rope(DA_HEAD_DIM ** -0.5 * math.log2(math.e)))
    proj(wk, ok, rope(1.0))
    proj(wv, ov)
    proj(wsg, osg)
    proj(wbr, obr)


def _inproj(x, g, cos, sin, ws, seq, tm):
    t = x.shape[0]
    row = lambda w: pl.BlockSpec((tm, w), lambda i: (i, 0))
    tiles_per_seq = seq // tm
    tab = pl.BlockSpec((tm, LANES), lambda i: (i % tiles_per_seq, 0))
    widths = [w.shape[1] for w in ws]
    dtypes = [BF16, BF16, F32, BF16, BF16, BF16, BF16, BF16]
    return pl.pallas_call(
        _inproj_body,
        grid=(t // tm,),
        in_specs=[row(D_MODEL), _const_spec((1, D_MODEL)), tab, tab] + [_const_spec(w.shape) for w in ws],
        out_specs=[row(w) for w in widths],
        out_shape=[jax.ShapeDtypeStruct((t, w), d) for w, d in zip(widths, dtypes)],
        scratch_shapes=[pltpu.VMEM((tm, D_MODEL), BF16)],
        name="in_proj",
        compiler_params=_cparams("parallel"),
    )(x, g, cos, sin, *ws)


def _dnprep_body(x_ref, xp_ref, xn_ref, gts_ref, cw_ref, alog_ref, dtb_ref, ones_ref, tri_ref,
                 q_ref, k_ref, v_ref, go_ref, ext_scr, *, tiles_per_seq):
    i = pl.program_id(0)
    tm = x_ref.shape[0]
    pos = i % tiles_per_seq
    half = HALO_ROWS // 2
    keep_prev = (pos != 0).astype(F32)
    keep_next = (pos != tiles_per_seq - 1).astype(F32)
    ext_scr[0:half, :] = xp_ref[...].astype(F32)[half:, :] * keep_prev
    ext_scr[half:half + tm, :] = x_ref[...].astype(F32)
    ext_scr[half + tm:HALO_ROWS + tm, :] = xn_ref[...].astype(F32)[:half, :] * keep_next

    ones = ones_ref[...]
    pad = DN_CONV // 2
    for part, o_ref in enumerate((q_ref, k_ref, v_ref)):
        ls = slice(DN_WIDTH * part, DN_WIDTH * (part + 1))
        acc = None
        for j in range(DN_CONV):
            off = half - pad + j
            term = ext_scr[off:off + tm, ls] * cw_ref[j:j + 1, ls]
            acc = term if acc is None else acc + term
        y = acc * jax.nn.sigmoid(acc)
        if part < 2:
            hi, lo = _split2(y * y)
            ss = _dot(hi, ones) + _dot(lo, ones)
            y = y * lax.rsqrt(ss + L2_EPS)
            if part == 0:
                y = y * (DN_HEAD_DIM ** -0.5)
        o_ref[...] = y.astype(o_ref.dtype)

    gl = gts_ref[...]
    lane = lax.broadcasted_iota(jnp.int32, (DN_CHUNK, GATE_LANES), 1)
    z = gl + dtb_ref[...]
    softplus = jnp.maximum(z, 0.0) + jnp.log1p(jnp.exp(-jnp.abs(z)))
    g = -jnp.exp(alog_ref[...]) * softplus
    beta = jax.nn.sigmoid(gl)
    tri = tri_ref[...]
    for c in range(tm // DN_CHUNK):
        rows = slice(c * DN_CHUNK, (c + 1) * DN_CHUNK)
        gs = g[rows]
        hi = gs.astype(BF16)
        r1 = gs - hi.astype(F32)
        mid = r1.astype(BF16)
        lo = (r1 - mid.astype(F32)).astype(BF16)
        cs = _dot(tri, jnp.concatenate([hi, mid, lo], axis=1))
        cs = cs[:, :GATE_LANES] + cs[:, GATE_LANES:2 * GATE_LANES] + cs[:, 2 * GATE_LANES:]
        fwd, bwd = cs[:DN_CHUNK], cs[DN_CHUNK:]
        go_ref[rows, :] = jnp.where(lane < DN_HEADS, fwd, jnp.where(lane < 2 * DN_HEADS, bwd, beta[rows]))


def _dnprep(qkv, gates, conv_w, alog, dtb, ones_bd, tri, seq, tm):
    t = qkv.shape[0]
    cw = qkv.shape[1]
    hb = tm // HALO_ROWS
    nhalo = t // HALO_ROWS
    tiles_per_seq = seq // tm
    out = pl.BlockSpec((tm, DN_WIDTH), lambda i: (i, 0))
    return pl.pallas_call(
        functools.partial(_dnprep_body, tiles_per_seq=tiles_per_seq),
        grid=(t // tm,),
        in_specs=[
            pl.BlockSpec((tm, cw), lambda i: (i, 0)),
            pl.BlockSpec((HALO_ROWS, cw), lambda i: (jnp.maximum(i * hb - 1, 0), 0)),
            pl.BlockSpec((HALO_ROWS, cw), lambda i: (jnp.minimum((i + 1) * hb, nhalo - 1), 0)),
            pl.BlockSpec((tm, GATE_LANES), lambda i: (i, 0)),
            _const_spec(conv_w.shape), _const_spec(alog.shape), _const_spec(dtb.shape),
            _const_spec(ones_bd.shape), _const_spec(tri.shape),
        ],
        out_specs=[out, out, out, pl.BlockSpec((tm, GATE_LANES), lambda i: (i, 0))],
        out_shape=[jax.ShapeDtypeStruct((t, DN_WIDTH), BF16)] * 3 + [jax.ShapeDtypeStruct((t, GATE_LANES), F32)],
        scratch_shapes=[pltpu.VMEM((tm + HALO_ROWS, cw), F32)],
        name="dn_prep",
        compiler_params=_cparams("parallel"),
    )(qkv, qkv, qkv, gates, conv_w, alog, dtb, ones_bd, tri)


def _blockdiag(x16, bd):
    return jnp.concatenate([x16] * DN_GROUP, axis=0) * bd


def _dn_parallel(probs, bd, eye):
    n = range(len(probs))
    q, k, v, gc, beta, strict, last_row, offdiag = (list(col) for col in zip(*probs))
    kf = [k[i].astype(F32) for i in n]
    kb = [kf[i] * beta[i] for i in n]
    pair = [_dot_nt(jnp.concatenate([kb[i].astype(BF16), q[i]], axis=0), _blockdiag(k[i], bd)) for i in n]
    gc_row = [jnp.sum(gc[i] * eye, axis=0, keepdims=True) for i in n]
    dec = [jnp.exp(jnp.where(strict[i] > 0.5, gc[i] - gc_row[i], -jnp.inf)) for i in n]
    qk = [(pair[i][DN_CHUNK:] * (dec[i] + eye)).astype(BF16) for i in n]

    gm = [pair[i][:DN_CHUNK] * dec[i] for i in n]
    for lvl in range(int(math.log2(DN_CHUNK))):
        g16 = [gm[i].astype(BF16) for i in n]
        gm = [gm[i] - _dot(g16[i] * offdiag[i][lvl], _blockdiag(g16[i], bd)) for i in n]

    egc = [jnp.exp(gc[i]) for i in n]
    vb = [v[i].astype(F32) * beta[i] for i in n]
    kbg = [kb[i] * egc[i] for i in n]
    uw = [jnp.concatenate([vb[i], kbg[i]], axis=1)
          - _dot(gm[i].astype(BF16), jnp.concatenate([_blockdiag(vb[i].astype(BF16), bd),
                                                      _blockdiag(kbg[i].astype(BF16), bd)], axis=1)) for i in n]
    g_last = [gc[i][last_row[i]:last_row[i] + 1, :] for i in n]
    out = []
    for i in n:
        kdec_t = (kf[i] * jnp.exp(g_last[i] - gc[i])).T.astype(BF16)
        out.append((uw[i][:, :DN_GROUP_LANES], uw[i][:, DN_GROUP_LANES:].astype(BF16),
                    (q[i].astype(F32) * egc[i]).astype(BF16), qk[i], kdec_t, jnp.exp(g_last[i])))
    return out


def _dn_sequential(s_refs, parts, bd):
    n = range(len(s_refs))
    u, w16, qg16, qk16, kdec_t, eg_last = (list(col) for col in zip(*parts))
    state = [s_refs[i][...] for i in n]
    ws = [_dot(jnp.concatenate([w16[i], qg16[i]], axis=0), state[i].astype(BF16)) for i in n]
    v_new = [(u[i] - ws[i][:DN_CHUNK]).astype(BF16) for i in n]
    o = [ws[i][DN_CHUNK:] + _dot(qk16[i], _blockdiag(v_new[i], bd)) for i in n]
    upd = [_dot(kdec_t[i], v_new[i]) for i in n]
    bdf = bd.astype(F32)
    for i in n:
        s_refs[i][...] = (state[i] * eg_last[i] + upd[i]) * bdf
    return o


def _dn_body(qf, kf, vf, gf, qb, kb, vb, gb, exp_ref, bd_ref, od_ref, of_ref, ob_ref, state, gexp, *, cb):
    i = pl.program_id(1)

    @pl.when(i == 0)
    def _():
        state[...] = jnp.zeros_like(state)

    for d, g_ref in enumerate((gf, gb)):
        hi, lo = _split2(g_ref[...])
        e = _dot(jnp.concatenate([hi, lo], axis=0), exp_ref[d])
        rows = g_ref.shape[0]
        gexp[d] = e[:rows] + e[rows:]

    bd = bd_ref[...]
    ri = lax.broadcasted_iota(jnp.int32, (DN_CHUNK, DN_GROUP_LANES), 0)
    cj = lax.broadcasted_iota(jnp.int32, (DN_CHUNK, DN_GROUP_LANES), 1) % DN_CHUNK
    eye = (ri == cj).astype(F32)
    lower = (ri > cj).astype(F32)
    upper = (ri < cj).astype(F32)
    gl = DN_GROUP_LANES
    s_refs = [state.at[idx] for idx in range(2 * DN_NGROUPS)]

    def body(c, carry):
        probs, dests = [], []
        for step in range(DN_UNROLL):
            pos = c * DN_UNROLL + step
            for d, (q_ref, k_ref, v_ref, o_ref) in enumerate(((qf, kf, vf, of_ref), (qb, kb, vb, ob_ref))):
                cc = pos if d == 0 else cb - 1 - pos
                rows = pl.ds(pl.multiple_of(cc * DN_CHUNK, DN_CHUNK), DN_CHUNK)
                for grp in range(DN_NGROUPS):
                    ls = slice(grp * gl, (grp + 1) * gl)
                    gc = gexp[d, rows, grp * gl:(grp + 1) * gl]
                    beta = gexp[d, rows, (DN_NGROUPS + grp) * gl:(DN_NGROUPS + grp + 1) * gl]
                    probs.append((q_ref[rows, ls], k_ref[rows, ls], v_ref[rows, ls], gc, beta,
                                  lower if d == 0 else upper, DN_CHUNK - 1 if d == 0 else 0, od_ref.at[d]))
                    dests.append((o_ref, rows, ls))
        parts = _dn_parallel(probs, bd, eye)
        ns = len(s_refs)
        for step in range(DN_UNROLL):
            outs = _dn_sequential(s_refs, parts[step * ns:(step + 1) * ns], bd)
            for o, (o_ref, rows, ls) in zip(outs, dests[step * ns:(step + 1) * ns]):
                o_ref[rows, ls] = o.astype(o_ref.dtype)
        return carry

    lax.fori_loop(0, cb // DN_UNROLL, body, 0)


def _dn_main(q, k, v, gates, expand, bd, offdiag, batch, seq, cb):
    t = q.shape[0]
    rows = cb * DN_CHUNK
    nb = seq // rows
    fwd = lambda w: pl.BlockSpec((rows, w), lambda b, i: (b * nb + i, 0))
    bwd = lambda w: pl.BlockSpec((rows, w), lambda b, i: (b * nb + nb - 1 - i, 0))
    return pl.pallas_call(
        functools.partial(_dn_body, cb=cb),
        grid=(batch, nb),
        in_specs=[fwd(DN_WIDTH)] * 3 + [fwd(GATE_LANES)] + [bwd(DN_WIDTH)] * 3 + [bwd(GATE_LANES)]
                 + [_const_spec(expand.shape), _const_spec(bd.shape), _const_spec(offdiag.shape)],
        out_specs=[fwd(DN_WIDTH), bwd(DN_WIDTH)],
        out_shape=[jax.ShapeDtypeStruct((t, DN_WIDTH), BF16)] * 2,
        scratch_shapes=[pltpu.VMEM((2 * DN_NGROUPS, DN_GROUP_LANES, DN_GROUP_LANES), F32),
                        pltpu.VMEM((2, rows, 2 * DN_NGROUPS * DN_GROUP_LANES), F32)],
        name="dn_delta_rule",
        compiler_params=_cparams("arbitrary", "arbitrary"),
    )(q, k, v, gates, q, k, v, gates, expand, bd, offdiag)


def _da_body(q_ref, k_ref, v_ref, lam_ref, g_ref, o_ref, s_scr, p_scr, m_scr, a_scr, acc_scr,
             *, tq, tk, lam_init):
    seq = k_ref.shape[0]
    nq = seq // tq
    nk = seq // tk
    assert nk % 2 == 0
    first_map = lax.broadcasted_iota(jnp.int32, (tq, LANES), 1) < DA_HEAD_DIM
    ones = jnp.ones((tk, LANES), BF16)
    lp = lam_ref[...]
    lam = (jnp.exp(jnp.sum(lp[0:1] * lp[1:2], axis=1, keepdims=True))
           - jnp.exp(jnp.sum(lp[2:3] * lp[3:4], axis=1, keepdims=True)) + lam_init)

    def key_rows(j):
        return pl.ds(pl.multiple_of(j * tk, tk), tk)

    def query_rows(qi):
        return pl.ds(pl.multiple_of(qi * tq, tq), tq)

    def qk_stage(qi, j, slot):
        q = q_ref[query_rows(qi), :]
        zero = jnp.zeros_like(q)
        q2 = jnp.concatenate([jnp.where(first_map, q, zero), jnp.where(first_map, zero, q)], axis=0)
        s_scr[slot] = _dot_nt(q2, k_ref[key_rows(j), :])

    def softmax_stage(slot, fresh=False):
        for r0 in range(0, 2 * tq, DA_SOFTMAX_ROWS):
            rows = slice(r0, r0 + DA_SOFTMAX_ROWS)
            s = s_scr[slot, rows, :]
            m_old = jnp.full((DA_SOFTMAX_ROWS, 1), -jnp.inf, F32) if fresh else m_scr[rows, :]
            m_new = jnp.maximum(m_old, jnp.max(s, axis=1, keepdims=True))
            p_scr[slot, rows, :] = jnp.exp2(s - m_new).astype(BF16)
            a_scr[slot, rows, :] = jnp.exp2(m_old - m_new)
            m_scr[rows, :] = m_new

    def pv_stage(j, slot):
        v_aug = jnp.concatenate([v_ref[key_rows(j), :], ones], axis=1)
        acc_scr[...] = a_scr[slot] * acc_scr[...] + _dot(p_scr[slot], v_aug)

    def finalize(qi):
        o = acc_scr[:, :LANES] / acc_scr[:, LANES:]
        od = o[:tq] - lam * o[tq:]
        o_ref[query_rows(qi), :] = (_rms(od, g_ref[...]) * (1.0 - lam_init)).astype(o_ref.dtype)

    acc_scr[...] = jnp.zeros(acc_scr.shape, F32)
    qk_stage(0, 0, 0)
    qk_stage(0, 1, 1)
    softmax_stage(0, fresh=True)

    def query_step(qi, carry):
        def pair(i, c):
            j = 2 * i
            qk_stage(qi, j + 2, 0)
            softmax_stage(1)
            pv_stage(j, 0)
            qk_stage(qi, j + 3, 1)
            softmax_stage(0)
            pv_stage(j + 1, 1)
            return c

        lax.fori_loop(0, nk // 2 - 1, pair, 0)
        nxt = jnp.minimum(qi + 1, nq - 1)
        qk_stage(nxt, 0, 0)
        softmax_stage(1)
        pv_stage(nk - 2, 0)
        qk_stage(nxt, 1, 1)
        pv_stage(nk - 1, 1)
        finalize(qi)
        softmax_stage(0, fresh=True)
        return carry

    lax.fori_loop(0, nq, query_step, 0)


def _diff_attn(q, k, v, lam, g, batch, seq, tq, tk, lam_init):
    t = q.shape[0]
    spec = pl.BlockSpec((seq, LANES), lambda b, h: (b, h))
    return pl.pallas_call(
        functools.partial(_da_body, tq=tq, tk=tk, lam_init=lam_init),
        grid=(batch, DA_HEADS),
        in_specs=[spec, spec, spec,
                  pl.BlockSpec(lam.shape, lambda b, h: (0, 0)),
                  pl.BlockSpec(g.shape, lambda b, h: (0, 0))],
        out_specs=spec,
        out_shape=jax.ShapeDtypeStruct((t, DA_WIDTH), BF16),
        scratch_shapes=[pltpu.VMEM((2, 2 * tq, tk), F32),
                        pltpu.VMEM((2, 2 * tq, tk), BF16),
                        pltpu.VMEM((2 * tq, 1), F32),
                        pltpu.VMEM((2, 2 * tq, 1), F32),
                        pltpu.VMEM((2 * tq, 2 * LANES), F32)],
        name="diff_attn",
        compiler_params=_cparams("parallel", "parallel"),
    )(q, k, v, lam, g)


def _gelu_tanh(x):
    return 0.5 * x * (1.0 + jnp.tanh(math.sqrt(2.0 / math.pi) * (x + 0.044715 * (x * x * x))))


def _sgu_body(uv_ref, lng_ref, lnb_ref, w_ref, bias_ref, bd_ref, o_ref):
    tm = uv_ref.shape[0]
    w = w_ref[...]
    bd = bd_ref[...]
    for c in range(tm // SG_CHUNK):
        rows = slice(c * SG_CHUNK, (c + 1) * SG_CHUNK)
        uv = _gelu_tanh(uv_ref[rows, :].astype(F32))
        u, v = uv[:, :SG_WIDTH], uv[:, SG_WIDTH:]
        mu = jnp.mean(v, axis=-1, keepdims=True)
        vc = v - mu
        vn = vc * lax.rsqrt(jnp.mean(vc * vc, axis=-1, keepdims=True) + NORM_EPS) * lng_ref[...] + lnb_ref[...]
        vbd = jnp.concatenate([vn.astype(BF16)] * SG_GROUPS, axis=0) * bd
        mixed = _dot(w, vbd) + bias_ref[...]
        o_ref[rows, :] = (u * mixed).astype(o_ref.dtype)


def _sgu(uv, ln_g, ln_b, wcat, bias, bd, tm):
    t = uv.shape[0]
    return pl.pallas_call(
        _sgu_body,
        grid=(t // tm,),
        in_specs=[pl.BlockSpec((tm, 2 * SG_WIDTH), lambda i: (i, 0)),
                  _const_spec(ln_g.shape), _const_spec(ln_b.shape), _const_spec(wcat.shape),
                  _const_spec(bias.shape), _const_spec(bd.shape)],
        out_specs=pl.BlockSpec((tm, SG_WIDTH), lambda i: (i, 0)),
        out_shape=jax.ShapeDtypeStruct((t, SG_WIDTH), BF16),
        name="spatial_gating",
        compiler_params=_cparams("parallel"),
    )(uv, ln_g, ln_b, wcat, bias, bd)


def _post_body(x_ref, of_ref, ob_ref, z_ref, oatt_ref, osg_ref, br_ref, dng_ref, ones_ref,
               wb_ref, wout_ref, gpost_ref, gpre2_ref, wup_ref, wdown_ref, gpost2_ref, y_ref):
    od = of_ref[...].astype(F32) + ob_ref[...].astype(F32)
    hi, lo = _split2(od * od)
    ones = ones_ref[...]
    ms = (_dot(hi, ones) + _dot(lo, ones)) * (1.0 / DN_HEAD_DIM)
    z = z_ref[...].astype(F32)
    o_a = od * lax.rsqrt(ms + NORM_EPS) * dng_ref[...] * (z * jax.nn.sigmoid(z))

    merged = None
    for idx, o_i in enumerate((o_a.astype(BF16), oatt_ref[...], osg_ref[...])):
        gate = jax.nn.sigmoid(br_ref[:, idx * D_MODEL:(idx + 1) * D_MODEL].astype(F32))
        term = gate * _dot(o_i, wb_ref[idx])
        merged = term if merged is None else merged + term
    mix = _dot(merged.astype(BF16), wout_ref[...])
    x1 = x_ref[...] + _rms(mix, gpost_ref[...])

    hm = _rms(x1, gpre2_ref[...]).astype(BF16)
    d_ff = wup_ref.shape[1]
    chunk = 1024
    f = None
    for c0 in range(0, d_ff, chunk):
        hid = jnp.maximum(_dot(hm, wup_ref[:, c0:c0 + chunk]), 0.0)
        term = _dot((hid * hid).astype(BF16), wdown_ref[c0:c0 + chunk, :])
        f = term if f is None else f + term
    y_ref[...] = x1 + _rms(f, gpost2_ref[...])


def _post(x, o_f, o_b, z, o_att, o_sg, br, consts, tm):
    t = x.shape[0]
    row = lambda w: pl.BlockSpec((tm, w), lambda i: (i, 0))
    return pl.pallas_call(
        _post_body,
        grid=(t // tm,),
        in_specs=[row(D_MODEL), row(DN_WIDTH), row(DN_WIDTH), row(DN_WIDTH), row(DA_WIDTH), row(SG_WIDTH),
                  row(N_BRANCH * D_MODEL)] + [_const_spec(c.shape) for c in consts],
        out_specs=row(D_MODEL),
        out_shape=jax.ShapeDtypeStruct((t, D_MODEL), F32),
        name="merge_mlp",
        compiler_params=_cparams("parallel"),
    )(x, o_f, o_b, z, o_att, o_sg, br, *consts)


def _np_consts():
    ones_bd = np.kron(np.eye(DN_HEADS), np.ones((DN_HEAD_DIM, DN_HEAD_DIM))).astype(np.float32)
    idx = np.arange(DN_CHUNK)
    tri = np.concatenate([idx[:, None] >= idx[None, :], idx[:, None] <= idx[None, :]], axis=0).astype(np.float32)
    bd4 = np.kron(np.eye(DN_GROUP), np.ones((DN_CHUNK, DN_HEAD_DIM))).astype(np.float32)
    bd8 = np.kron(np.eye(SG_GROUPS), np.ones((SG_CHUNK, SG_GROUP_DIM))).astype(np.float32)
    expand = np.zeros((2, GATE_LANES, 2 * DN_NGROUPS * DN_GROUP_LANES), np.float32)
    for d in range(2):
        for kind in range(2):
            for h in range(DN_HEADS):
                src = kind * 2 * DN_HEADS + d * DN_HEADS + h
                dst = kind * DN_WIDTH + h * DN_HEAD_DIM
                expand[d, src, dst:dst + DN_HEAD_DIM] = 1.0
    levels = int(math.log2(DN_CHUNK))
    offdiag = np.zeros((2, levels, DN_CHUNK, DN_CHUNK), np.float32)
    for lvl in range(levels):
        s = 1 << lvl
        same = (idx[:, None] // (2 * s)) == (idx[None, :] // (2 * s))
        hi_half, lo_half = (idx // s) % 2 == 1, (idx // s) % 2 == 0
        offdiag[0, lvl] = same & hi_half[:, None] & lo_half[None, :]
        offdiag[1, lvl] = same & lo_half[:, None] & hi_half[None, :]
    offdiag = np.tile(offdiag, (1, 1, 1, DN_GROUP))
    as16 = lambda a: jnp.asarray(a, dtype=BF16)
    return as16(ones_bd), as16(tri), as16(bd4), as16(bd8), as16(expand), as16(offdiag)


def _layer_weights(l, g_mix_pre, w_in, conv_w, dn_A_log, dn_dt_bias, dn_norm_g, da_lambda, da_subln_g,
                   sg_ln_g, sg_ln_b, sg_w, sg_b, w_branch, w_out, g_mix_post, g_mlp_pre, w_up, w_down,
                   g_mlp_post):
    sizes = (DN_WIDTH, DN_WIDTH, DN_WIDTH, DN_WIDTH, 4 * DN_HEADS, DA_WIDTH, DA_WIDTH, DA_WIDTH,
             2 * SG_WIDTH, N_BRANCH * D_MODEL)
    offs = np.concatenate([[0], np.cumsum(sizes)])
    w = w_in[l].astype(BF16)
    col = lambda a, b: w[:, int(offs[a]):int(offs[b])]
    w_gates = jnp.pad(col(4, 5), ((0, 0), (0, GATE_LANES - 4 * DN_HEADS)))
    in_ws = [col(0, 3), col(3, 4), w_gates, col(5, 6), col(6, 7), col(7, 8), col(8, 9), col(9, 10)]
    row = lambda v: v.astype(F32).reshape(1, -1)
    pad_lanes = lambda v: jnp.pad(v.astype(F32).reshape(1, -1), ((0, 0), (0, GATE_LANES - 2 * DN_HEADS)))
    return dict(
        g_pre=row(g_mix_pre[l]), in_ws=in_ws,
        conv_w=conv_w[l].astype(F32), alog=pad_lanes(dn_A_log[l]), dtb=pad_lanes(dn_dt_bias[l]),
        lam=da_lambda[l].astype(F32), subln=row(da_subln_g[l]),
        ln_g=row(sg_ln_g[l]), ln_b=row(sg_ln_b[l]),
        sg_wcat=jnp.transpose(sg_w[l], (1, 0, 2)).reshape(SG_CHUNK, SG_GROUPS * SG_CHUNK).astype(BF16),
        sg_bias=jnp.repeat(sg_b[l].astype(F32).T, SG_GROUP_DIM, axis=1),
        dn_g=jnp.tile(dn_norm_g[l].astype(F32), DN_HEADS)[None, :],
        wb=w_branch[l].astype(BF16), wout=w_out[l].astype(BF16), g_post=row(g_mix_post[l]),
        g_pre2=row(g_mlp_pre[l]), wup=w_up[l].astype(BF16), wdown=w_down[l].astype(BF16),
        g_post2=row(g_mlp_post[l]),
    )


def _lambda_init(layer):
    return 0.8 - 0.6 * math.exp(-0.3 * layer)


def _tile(n, pref):
    return pref if n % pref == 0 else n


def _run_layer(x, lw, lam_init, cos, sin, consts, batch, seq):
    ones_bd, tri, bd4, bd8, expand, offdiag = consts
    t = batch * seq
    qkv, z, gates, q_rot, k_rot, v_da, sg_uv, br = _inproj(
        x, lw["g_pre"], cos, sin, lw["in_ws"], seq, _tile(seq, 512))
    dq, dk, dv, dgates = _dnprep(qkv, gates, lw["conv_w"], lw["alog"], lw["dtb"], ones_bd, tri,
                                 seq, _tile(seq, 256))
    o_f, o_b = _dn_main(dq, dk, dv, dgates, expand, bd4, offdiag, batch, seq, _tile(seq // DN_CHUNK, 16))
    o_att = _diff_attn(q_rot, k_rot, v_da, lw["lam"], lw["subln"], batch, seq,
                       _tile(seq, 256), min(1024, seq // 2), lam_init)
    o_sg = _sgu(sg_uv, lw["ln_g"], lw["ln_b"], lw["sg_wcat"], lw["sg_bias"], bd8, _tile(t, 512))
    post_consts = (lw["dn_g"], ones_bd, lw["wb"], lw["wout"], lw["g_post"], lw["g_pre2"], lw["wup"],
                   lw["wdown"], lw["g_post2"])
    return _post(x, o_f, o_b, z, o_att, o_sg, br, post_consts, _tile(t, 512))


def kernel(x_prompt, x_sample, g_mix_pre, w_in, conv_w, dn_A_log, dn_dt_bias, dn_norm_g, da_lambda, da_subln_g,
           sg_ln_g, sg_ln_b, sg_w, sg_b, w_branch, w_out, g_mix_post, g_mlp_pre, w_up, w_down, g_mlp_post):
    params = (g_mix_pre, w_in, conv_w, dn_A_log, dn_dt_bias, dn_norm_g, da_lambda, da_subln_g, sg_ln_g, sg_ln_b,
              sg_w, sg_b, w_branch, w_out, g_mix_post, g_mlp_pre, w_up, w_down, g_mlp_post)
    depth = w_in.shape[0]
    consts = _np_consts()
    layers = [_layer_weights(l, *params) for l in range(depth)]
    cos, sin = _rope_tables(max(x_prompt.shape[1], x_sample.shape[1]))

    def trunk(x):
        batch, seq, _ = x.shape
        h = x.astype(F32).reshape(batch * seq, D_MODEL)
        for l in range(depth):
            h = _run_layer(h, layers[l], _lambda_init(l), cos, sin, consts, batch, seq)
        return h.reshape(batch, seq, D_MODEL)

    return trunk(x_prompt), trunk(x_sample)
```

```python
import functools
import math

import numpy as np
import jax
import jax.numpy as jnp
from jax import lax
from jax.experimental import pallas as pl
from jax.experimental.pallas import tpu as pltpu

F32 = jnp.float32
BF16 = jnp.bfloat16

D_MODEL = 1024
DN_HEADS = 8
DN_HEAD_DIM = 64
DN_WIDTH = DN_HEADS * DN_HEAD_DIM
DN_CONV = 5
DN_CHUNK = 64
DA_HEADS = 4
DA_HEAD_DIM = 64
DA_WIDTH = DA_HEADS * 2 * DA_HEAD_DIM
ROPE_THETA = 10000.0
SG_GROUPS = 8
SG_WIDTH = 512
SG_GROUP_DIM = SG_WIDTH // SG_GROUPS
SG_CHUNK = 128
N_BRANCH = 3
NORM_EPS = 1e-6
L2_EPS = 1e-6

LANES = 128
GATE_LANES = LANES
DN_GROUP = 4
DN_GROUP_LANES = DN_GROUP * DN_HEAD_DIM
DN_NGROUPS = DN_HEADS // DN_GROUP
HALO_ROWS = 16
DA_SOFTMAX_ROWS = 32
DN_UNROLL = 2

VMEM_LIMIT = 56 * 2**20


def _cparams(*sem):
    return pltpu.CompilerParams(dimension_semantics=sem, vmem_limit_bytes=VMEM_LIMIT)


def _const_spec(shape):
    nd = len(shape)
    return pl.BlockSpec(shape, lambda *_: (0,) * nd, pipeline_mode=pl.Buffered(1))


def _dot(a, b):
    return jnp.dot(a, b, preferred_element_type=F32)


def _dot_nt(a, b):
    return lax.dot_general(a, b, (((1,), (1,)), ((), ())), preferred_element_type=F32)


def _split2(x):
    hi = x.astype(BF16)
    lo = (x - hi.astype(F32)).astype(BF16)
    return hi, lo


def _rms(x, g):
    return x * lax.rsqrt(jnp.mean(x * x, axis=-1, keepdims=True) + NORM_EPS) * g


def _rope_body(inv_ref, cos_ref, sin_ref):
    tb = cos_ref.shape[0]
    row = lax.broadcasted_iota(jnp.int32, (tb, LANES), 0) + pl.program_id(0) * tb
    lane = lax.broadcasted_iota(jnp.int32, (tb, LANES), 1)
    ang = row.astype(F32) * inv_ref[...]
    half = DA_HEAD_DIM // 2
    sign = jnp.where(lane % DA_HEAD_DIM < half, -1.0, 1.0)
    cos_ref[...] = jnp.cos(ang)
    sin_ref[...] = jnp.sin(ang) * sign


def _rope_tables(seq):
    half = DA_HEAD_DIM // 2
    inv = jnp.power(ROPE_THETA, -jnp.arange(half, dtype=F32) / half)
    inv = jnp.tile(inv, LANES // half)[None, :]
    tb = min(seq, 1024)
    return pl.pallas_call(
        _rope_body,
        grid=(seq // tb,),
        in_specs=[pl.BlockSpec((1, LANES), lambda i: (0, 0))],
        out_specs=[pl.BlockSpec((tb, LANES), lambda i: (i, 0))] * 2,
        out_shape=[jax.ShapeDtypeStruct((seq, LANES), F32)] * 2,
        name="rope_tables",
        compiler_params=_cparams("parallel"),
    )(inv)


def _inproj_body(x_ref, xp_ref, xn_ref, g_ref, cos_ref, sin_ref, cw_ref, alog_ref, dtb_ref, ones_ref, tri_ref,
                 wqkv, wz, wg, wq, wk, wv, wsg, wbr,
                 odq, odk, odv, oz, og, oq, ok, ov, osg, obr, h_scr, ext_scr, *, tiles_per_seq):
    tm = x_ref.shape[0]
    halo = HALO_ROWS // 2
    h_scr[0:tm, :] = _rms(x_ref[...], g_ref[...]).astype(BF16)
    h_scr[tm:tm + halo, :] = _rms(xp_ref[...], g_ref[...]).astype(BF16)
    h_scr[tm + halo:tm + 2 * halo, :] = _rms(xn_ref[...], g_ref[...]).astype(BF16)
    chunk = 512

    def proj(w_ref, o_ref, epilogue=None):
        width = w_ref.shape[1]
        step = min(chunk, width)
        for c0 in range(0, width, step):
            y = _dot(h_scr[0:tm, :], w_ref[:, c0:c0 + step])
            if epilogue is not None:
                y = epilogue(y)
            o_ref[:, c0:c0 + step] = y.astype(o_ref.dtype)

    pos = pl.program_id(0) % tiles_per_seq
    keep_prev = (pos != 0).astype(F32)
    keep_next = (pos != tiles_per_seq - 1).astype(F32)
    for c0 in range(0, wqkv.shape[1], chunk):
        y = _dot(h_scr[...], wqkv[:, c0:c0 + chunk])
        ext_scr[halo:halo + tm, c0:c0 + chunk] = y[0:tm]
        ext_scr[0:halo, c0:c0 + chunk] = y[tm:tm + halo] * keep_prev
        ext_scr[halo + tm:2 * halo + tm, c0:c0 + chunk] = y[tm + halo:] * keep_next
    _dn_conv_silu(ext_scr, cw_ref, odv, tm)

    reps = DA_WIDTH // LANES
    cos = jnp.concatenate([cos_ref[...]] * reps, axis=1)
    sin = jnp.concatenate([sin_ref[...]] * reps, axis=1)
    lane = lax.broadcasted_iota(jnp.int32, (tm, DA_WIDTH), 1)
    first_half = lane % DA_HEAD_DIM < DA_HEAD_DIM // 2
    half = DA_HEAD_DIM // 2

    def rope(scale):
        def f(y):
            up = pltpu.roll(y, DA_WIDTH - half, axis=1)
            dn = pltpu.roll(y, half, axis=1)
            out = y * cos + jnp.where(first_half, up, dn) * sin
            return out * scale if scale != 1.0 else out
        return f

    proj(wz, oz)
    _dn_gates(_dot(h_scr[0:tm, :], wg[...]), alog_ref[...], dtb_ref[...], tri_ref[...], og)
    proj(wq, oq, rope(DA_HEAD_DIM ** -0.5 * math.log2(math.e)))
    proj(wk, ok, rope(1.0))
    proj(wv, ov)
    proj(wsg, osg)
    proj(wbr, obr)
    _dn_l2norm(ext_scr, ones_ref[...], (odq, odk), tm)


def _inproj(x, g, cos, sin, ws, dn_consts, seq, tm):
    t = x.shape[0]
    row = lambda w: pl.BlockSpec((tm, w), lambda i: (i, 0))
    tiles_per_seq = seq // tm
    halo = HALO_ROWS // 2
    hb = tm // halo
    nhalo = t // halo
    tab = pl.BlockSpec((tm, LANES), lambda i: (i % tiles_per_seq, 0))
    widths = [DN_WIDTH] * 3 + [w.shape[1] for w in ws[1:]]
    dtypes = [BF16, BF16, BF16, BF16, F32, BF16, BF16, BF16, BF16, BF16]
    return pl.pallas_call(
        functools.partial(_inproj_body, tiles_per_seq=tiles_per_seq),
        grid=(t // tm,),
        in_specs=[row(D_MODEL),
                  pl.BlockSpec((halo, D_MODEL), lambda i: (jnp.maximum(i * hb - 1, 0), 0)),
                  pl.BlockSpec((halo, D_MODEL), lambda i: (jnp.minimum((i + 1) * hb, nhalo - 1), 0)),
                  _const_spec((1, D_MODEL)), tab, tab]
                 + [_const_spec(c.shape) for c in dn_consts] + [_const_spec(w.shape) for w in ws],
        out_specs=[row(w) for w in widths],
        out_shape=[jax.ShapeDtypeStruct((t, w), d) for w, d in zip(widths, dtypes)],
        scratch_shapes=[pltpu.VMEM((tm + HALO_ROWS, D_MODEL), BF16),
                        pltpu.VMEM((tm + HALO_ROWS, ws[0].shape[1]), F32)],
        name="in_proj",
        compiler_params=_cparams("parallel"),
    )(x, x, x, g, cos, sin, *dn_consts, *ws)


def _dn_conv_silu(ext_scr, cw_ref, v_ref, tm):
    half = HALO_ROWS // 2
    pad = DN_CONV // 2
    for part in range(3):
        ls = slice(DN_WIDTH * part, DN_WIDTH * (part + 1))
        acc = None
        for j in range(DN_CONV):
            off = half - pad + j
            term = ext_scr[off:off + tm, ls] * cw_ref[j:j + 1, ls]
            acc = term if acc is None else acc + term
        y = acc * jax.nn.sigmoid(acc)
        if part < 2:
            ext_scr[half:half + tm, ls] = y
        else:
            v_ref[...] = y.astype(v_ref.dtype)


def _dn_l2norm(ext_scr, ones, o_refs, tm):
    half = HALO_ROWS // 2
    for part, o_ref in enumerate(o_refs):
        y = ext_scr[half:half + tm, DN_WIDTH * part:DN_WIDTH * (part + 1)]
        ss = _dot((y * y).astype(BF16), ones)
        y = y * lax.rsqrt(ss + L2_EPS)
        if part == 0:
            y = y * (DN_HEAD_DIM ** -0.5)
        o_ref[...] = y.astype(o_ref.dtype)


def _dn_gates(gl, alog, dtb, tri, go_ref):
    tm = gl.shape[0]
    lane = lax.broadcasted_iota(jnp.int32, (DN_CHUNK, GATE_LANES), 1)
    z = gl + dtb
    softplus = jnp.maximum(z, 0.0) + jnp.log1p(jnp.exp(-jnp.abs(z)))
    g = -jnp.exp(alog) * softplus
    beta = jax.nn.sigmoid(gl)
    for c in range(tm // DN_CHUNK):
        rows = slice(c * DN_CHUNK, (c + 1) * DN_CHUNK)
        gs = g[rows]
        hi = gs.astype(BF16)
        r1 = gs - hi.astype(F32)
        mid = r1.astype(BF16)
        lo = (r1 - mid.astype(F32)).astype(BF16)
        cs = _dot(tri, jnp.concatenate([hi, mid, lo], axis=1))
        cs = cs[:, :GATE_LANES] + cs[:, GATE_LANES:2 * GATE_LANES] + cs[:, 2 * GATE_LANES:]
        fwd, bwd = cs[:DN_CHUNK], cs[DN_CHUNK:]
        go_ref[rows, :] = jnp.where(lane < DN_HEADS, fwd, jnp.where(lane < 2 * DN_HEADS, bwd, beta[rows]))


def _blockdiag(x16, bd):
    return jnp.concatenate([x16] * DN_GROUP, axis=0) * bd


def _dn_parallel(probs, bd, eye):
    n = range(len(probs))
    q, k, v, gc, beta, strict, last_row, offdiag = (list(col) for col in zip(*probs))
    kf = [k[i].astype(F32) for i in n]
    kb = [kf[i] * beta[i] for i in n]
    pair = [_dot_nt(jnp.concatenate([kb[i].astype(BF16), q[i]], axis=0), _blockdiag(k[i], bd)) for i in n]
    gc_row = [jnp.sum(gc[i] * eye, axis=0, keepdims=True) for i in n]
    dec = [jnp.exp(jnp.where(strict[i] > 0.5, gc[i] - gc_row[i], -jnp.inf)) for i in n]
    qk = [(pair[i][DN_CHUNK:] * (dec[i] + eye)).astype(BF16) for i in n]

    gm = [pair[i][:DN_CHUNK] * dec[i] for i in n]
    for lvl in range(int(math.log2(DN_CHUNK))):
        g16 = [gm[i].astype(BF16) for i in n]
        gm = [gm[i] - _dot(g16[i] * offdiag[i][lvl], _blockdiag(g16[i], bd)) for i in n]

    egc = [jnp.exp(gc[i]) for i in n]
    vb = [v[i].astype(F32) * beta[i] for i in n]
    kbg = [kb[i] * egc[i] for i in n]
    uw = [jnp.concatenate([vb[i], kbg[i]], axis=1)
          - _dot(gm[i].astype(BF16), jnp.concatenate([_blockdiag(vb[i].astype(BF16), bd),
                                                      _blockdiag(kbg[i].astype(BF16), bd)], axis=1)) for i in n]
    g_last = [gc[i][last_row[i]:last_row[i] + 1, :] for i in n]
    out = []
    for i in n:
        kdec_t = (kf[i] * jnp.exp(g_last[i] - gc[i])).T.astype(BF16)
        out.append((uw[i][:, :DN_GROUP_LANES], uw[i][:, DN_GROUP_LANES:].astype(BF16),
                    (q[i].astype(F32) * egc[i]).astype(BF16), qk[i], kdec_t, jnp.exp(g_last[i])))
    return out


def _dn_sequential(s_refs, parts, bd):
    n = range(len(s_refs))
    u, w16, qg16, qk16, kdec_t, eg_last = (list(col) for col in zip(*parts))
    state = [s_refs[i][...] for i in n]
    ws = [_dot(jnp.concatenate([w16[i], qg16[i]], axis=0), state[i].astype(BF16)) for i in n]
    v_new = [(u[i] - ws[i][:DN_CHUNK]).astype(BF16) for i in n]
    o = [ws[i][DN_CHUNK:] + _dot(qk16[i], _blockdiag(v_new[i], bd)) for i in n]
    upd = [_dot(kdec_t[i], v_new[i]) for i in n]
    bdf = bd.astype(F32)
    for i in n:
        s_refs[i][...] = (state[i] * eg_last[i] + upd[i]) * bdf
    return o


def _dn_body(qf, kf, vf, gf, qb, kb, vb, gb, exp_ref, bd_ref, od_ref, of_ref, ob_ref, state, gexp, *, cb):
    i = pl.program_id(1)

    @pl.when(i == 0)
    def _():
        state[...] = jnp.zeros_like(state)

    for d, g_ref in enumerate((gf, gb)):
        hi, lo = _split2(g_ref[...])
        e = _dot(jnp.concatenate([hi, lo], axis=0), exp_ref[d])
        rows = g_ref.shape[0]
        gexp[d] = e[:rows] + e[rows:]

    bd = bd_ref[...]
    ri = lax.broadcasted_iota(jnp.int32, (DN_CHUNK, DN_GROUP_LANES), 0)
    cj = lax.broadcasted_iota(jnp.int32, (DN_CHUNK, DN_GROUP_LANES), 1) % DN_CHUNK
    eye = (ri == cj).astype(F32)
    lower = (ri > cj).astype(F32)
    upper = (ri < cj).astype(F32)
    gl = DN_GROUP_LANES
    s_refs = [state.at[idx] for idx in range(2 * DN_NGROUPS)]

    def body(c, carry):
        probs, dests = [], []
        for step in range(DN_UNROLL):
            pos = c * DN_UNROLL + step
            for d, (q_ref, k_ref, v_ref, o_ref) in enumerate(((qf, kf, vf, of_ref), (qb, kb, vb, ob_ref))):
                cc = pos if d == 0 else cb - 1 - pos
                rows = pl.ds(pl.multiple_of(cc * DN_CHUNK, DN_CHUNK), DN_CHUNK)
                for grp in range(DN_NGROUPS):
                    ls = slice(grp * gl, (grp + 1) * gl)
                    gc = gexp[d, rows, grp * gl:(grp + 1) * gl]
                    beta = gexp[d, rows, (DN_NGROUPS + grp) * gl:(DN_NGROUPS + grp + 1) * gl]
                    probs.append((q_ref[rows, ls], k_ref[rows, ls], v_ref[rows, ls], gc, beta,
                                  lower if d == 0 else upper, DN_CHUNK - 1 if d == 0 else 0, od_ref.at[d]))
                    dests.append((o_ref, rows, ls))
        parts = _dn_parallel(probs, bd, eye)
        ns = len(s_refs)
        for step in range(DN_UNROLL):
            outs = _dn_sequential(s_refs, parts[step * ns:(step + 1) * ns], bd)
            for o, (o_ref, rows, ls) in zip(outs, dests[step * ns:(step + 1) * ns]):
                o_ref[rows, ls] = o.astype(o_ref.dtype)
        return carry

    lax.fori_loop(0, cb // DN_UNROLL, body, 0)


def _dn_main(q, k, v, gates, expand, bd, offdiag, batch, seq, cb):
    t = q.shape[0]
    rows = cb * DN_CHUNK
    nb = seq // rows
    fwd = lambda w: pl.BlockSpec((rows, w), lambda b, i: (b * nb + i, 0))
    bwd = lambda w: pl.BlockSpec((rows, w), lambda b, i: (b * nb + nb - 1 - i, 0))
    return pl.pallas_call(
        functools.partial(_dn_body, cb=cb),
        grid=(batch, nb),
        in_specs=[fwd(DN_WIDTH)] * 3 + [fwd(GATE_LANES)] + [bwd(DN_WIDTH)] * 3 + [bwd(GATE_LANES)]
                 + [_const_spec(expand.shape), _const_spec(bd.shape), _const_spec(offdiag.shape)],
        out_specs=[fwd(DN_WIDTH), bwd(DN_WIDTH)],
        out_shape=[jax.ShapeDtypeStruct((t, DN_WIDTH), BF16)] * 2,
        scratch_shapes=[pltpu.VMEM((2 * DN_NGROUPS, DN_GROUP_LANES, DN_GROUP_LANES), F32),
                        pltpu.VMEM((2, rows, 2 * DN_NGROUPS * DN_GROUP_LANES), F32)],
        name="dn_delta_rule",
        compiler_params=_cparams("arbitrary", "arbitrary"),
    )(q, k, v, gates, q, k, v, gates, expand, bd, offdiag)


def _da_body(q_ref, k_ref, v_ref, lam_ref, g_ref, o_ref, s_scr, p_scr, m_scr, a_scr, acc_scr,
             *, tq, tk, lam_init):
    seq = k_ref.shape[0]
    nq = seq // tq
    nk = seq // tk
    assert nk % 2 == 0
    first_map = lax.broadcasted_iota(jnp.int32, (tq, LANES), 1) < DA_HEAD_DIM
    ones = jnp.ones((tk, LANES), BF16)
    lp = lam_ref[...]
    lam = (jnp.exp(jnp.sum(lp[0:1] * lp[1:2], axis=1, keepdims=True))
           - jnp.exp(jnp.sum(lp[2:3] * lp[3:4], axis=1, keepdims=True)) + lam_init)

    def key_rows(j):
        return pl.ds(pl.multiple_of(j * tk, tk), tk)

    def query_rows(qi):
        return pl.ds(pl.multiple_of(qi * tq, tq), tq)

    def qk_stage(qi, j, slot):
        q = q_ref[query_rows(qi), :]
        zero = jnp.zeros_like(q)
        q2 = jnp.concatenate([jnp.where(first_map, q, zero), jnp.where(first_map, zero, q)], axis=0)
        s_scr[slot] = _dot_nt(q2, k_ref[key_rows(j), :])

    def softmax_stage(slot, fresh=False):
        for r0 in range(0, 2 * tq, DA_SOFTMAX_ROWS):
            rows = slice(r0, r0 + DA_SOFTMAX_ROWS)
            s = s_scr[slot, rows, :]
            m_old = jnp.full((DA_SOFTMAX_ROWS, 1), -jnp.inf, F32) if fresh else m_scr[rows, :]
            m_new = jnp.maximum(m_old, jnp.max(s, axis=1, keepdims=True))
            p_scr[slot, rows, :] = jnp.exp2(s - m_new).astype(BF16)
            a_scr[slot, rows, :] = jnp.exp2(m_old - m_new)
            m_scr[rows, :] = m_new

    def pv_stage(j, slot):
        v_aug = jnp.concatenate([v_ref[key_rows(j), :], ones], axis=1)
        acc_scr[...] = a_scr[slot] * acc_scr[...] + _dot(p_scr[slot], v_aug)

    def finalize(qi):
        o = acc_scr[:, :LANES] / acc_scr[:, LANES:]
        od = o[:tq] - lam * o[tq:]
        o_ref[query_rows(qi), :] = (_rms(od, g_ref[...]) * (1.0 - lam_init)).astype(o_ref.dtype)

    acc_scr[...] = jnp.zeros(acc_scr.shape, F32)
    qk_stage(0, 0, 0)
    qk_stage(0, 1, 1)
    softmax_stage(0, fresh=True)

    def query_step(qi, carry):
        def pair(i, c):
            j = 2 * i
            qk_stage(qi, j + 2, 0)
            softmax_stage(1)
            pv_stage(j, 0)
            qk_stage(qi, j + 3, 1)
            softmax_stage(0)
            pv_stage(j + 1, 1)
            return c

        lax.fori_loop(0, nk // 2 - 1, pair, 0)
        nxt = jnp.minimum(qi + 1, nq - 1)
        qk_stage(nxt, 0, 0)
        softmax_stage(1)
        pv_stage(nk - 2, 0)
        qk_stage(nxt, 1, 1)
        pv_stage(nk - 1, 1)
        finalize(qi)
        softmax_stage(0, fresh=True)
        return carry

    lax.fori_loop(0, nq, query_step, 0)


def _diff_attn(q, k, v, lam, g, batch, seq, tq, tk, lam_init):
    t = q.shape[0]
    spec = pl.BlockSpec((seq, LANES), lambda b, h: (b, h))
    return pl.pallas_call(
        functools.partial(_da_body, tq=tq, tk=tk, lam_init=lam_init),
        grid=(batch, DA_HEADS),
        in_specs=[spec, spec, spec,
                  pl.BlockSpec(lam.shape, lambda b, h: (0, 0)),
                  pl.BlockSpec(g.shape, lambda b, h: (0, 0))],
        out_specs=spec,
        out_shape=jax.ShapeDtypeStruct((t, DA_WIDTH), BF16),
        scratch_shapes=[pltpu.VMEM((2, 2 * tq, tk), F32),
                        pltpu.VMEM((2, 2 * tq, tk), BF16),
                        pltpu.VMEM((2 * tq, 1), F32),
                        pltpu.VMEM((2, 2 * tq, 1), F32),
                        pltpu.VMEM((2 * tq, 2 * LANES), F32)],
        name="diff_attn",
        compiler_params=_cparams("parallel", "parallel"),
    )(q, k, v, lam, g)


def _gelu_tanh(x):
    return 0.5 * x * (1.0 + jnp.tanh(math.sqrt(2.0 / math.pi) * (x + 0.044715 * (x * x * x))))


def _spatial_gating(uv_ref, lng_ref, lnb_ref, w_ref, bias_ref, bd_ref, o_scr):
    tm = uv_ref.shape[0]
    w = w_ref[...]
    bd = bd_ref[...]
    for c in range(tm // SG_CHUNK):
        rows = slice(c * SG_CHUNK, (c + 1) * SG_CHUNK)
        uv = _gelu_tanh(uv_ref[rows, :].astype(F32))
        u, v = uv[:, :SG_WIDTH], uv[:, SG_WIDTH:]
        mu = jnp.mean(v, axis=-1, keepdims=True)
        vc = v - mu
        vn = vc * lax.rsqrt(jnp.mean(vc * vc, axis=-1, keepdims=True) + NORM_EPS) * lng_ref[...] + lnb_ref[...]
        vbd = jnp.concatenate([vn.astype(BF16)] * SG_GROUPS, axis=0) * bd
        mixed = _dot(w, vbd) + bias_ref[...]
        o_scr[rows, :] = (u * mixed).astype(o_scr.dtype)


def _post_body(x_ref, of_ref, ob_ref, z_ref, oatt_ref, uv_ref, br_ref, lng_ref, lnb_ref, sgw_ref, sgb_ref,
               bd8_ref, dng_ref, ones_ref, wb_ref, wout_ref, gpost_ref, gpre2_ref, wup_ref, wdown_ref,
               gpost2_ref, y_ref, osg_scr):
    _spatial_gating(uv_ref, lng_ref, lnb_ref, sgw_ref, sgb_ref, bd8_ref, osg_scr)
    od = of_ref[...].astype(F32) + ob_ref[...].astype(F32)
    ms = _dot((od * od).astype(BF16), ones_ref[...]) * (1.0 / DN_HEAD_DIM)
    z = z_ref[...].astype(F32)
    o_a = od * lax.rsqrt(ms + NORM_EPS) * dng_ref[...] * (z * jax.nn.sigmoid(z))

    merged = None
    for idx, o_i in enumerate((o_a.astype(BF16), oatt_ref[...], osg_scr[...])):
        gate = jax.nn.sigmoid(br_ref[:, idx * D_MODEL:(idx + 1) * D_MODEL].astype(F32))
        term = gate * _dot(o_i, wb_ref[idx])
        merged = term if merged is None else merged + term
    mix = _dot(merged.astype(BF16), wout_ref[...])
    x1 = x_ref[...] + _rms(mix, gpost_ref[...])

    hm = _rms(x1, gpre2_ref[...]).astype(BF16)
    d_ff = wup_ref.shape[1]
    chunk = 1024
    f = None
    for c0 in range(0, d_ff, chunk):
        hid = jnp.maximum(_dot(hm, wup_ref[:, c0:c0 + chunk]), 0.0)
        term = _dot((hid * hid).astype(BF16), wdown_ref[c0:c0 + chunk, :])
        f = term if f is None else f + term
    y_ref[...] = x1 + _rms(f, gpost2_ref[...])


def _post(x, o_f, o_b, z, o_att, sg_uv, br, consts, tm):
    t = x.shape[0]
    row = lambda w: pl.BlockSpec((tm, w), lambda i: (i, 0))
    return pl.pallas_call(
        _post_body,
        grid=(t // tm,),
        in_specs=[row(D_MODEL), row(DN_WIDTH), row(DN_WIDTH), row(DN_WIDTH), row(DA_WIDTH), row(2 * SG_WIDTH),
                  row(N_BRANCH * D_MODEL)] + [_const_spec(c.shape) for c in consts],
        out_specs=row(D_MODEL),
        out_shape=jax.ShapeDtypeStruct((t, D_MODEL), F32),
        scratch_shapes=[pltpu.VMEM((tm, SG_WIDTH), BF16)],
        name="merge_mlp",
        compiler_params=_cparams("parallel"),
    )(x, o_f, o_b, z, o_att, sg_uv, br, *consts)


def _np_consts():
    ones_bd = np.kron(np.eye(DN_HEADS), np.ones((DN_HEAD_DIM, DN_HEAD_DIM))).astype(np.float32)
    idx = np.arange(DN_CHUNK)
    tri = np.concatenate([idx[:, None] >= idx[None, :], idx[:, None] <= idx[None, :]], axis=0).astype(np.float32)
    bd4 = np.kron(np.eye(DN_GROUP), np.ones((DN_CHUNK, DN_HEAD_DIM))).astype(np.float32)
    bd8 = np.kron(np.eye(SG_GROUPS), np.ones((SG_CHUNK, SG_GROUP_DIM))).astype(np.float32)
    expand = np.zeros((2, GATE_LANES, 2 * DN_NGROUPS * DN_GROUP_LANES), np.float32)
    for d in range(2):
        for kind in range(2):
            for h in range(DN_HEADS):
                src = kind * 2 * DN_HEADS + d * DN_HEADS + h
                dst = kind * DN_WIDTH + h * DN_HEAD_DIM
                expand[d, src, dst:dst + DN_HEAD_DIM] = 1.0
    levels = int(math.log2(DN_CHUNK))
    offdiag = np.zeros((2, levels, DN_CHUNK, DN_CHUNK), np.float32)
    for lvl in range(levels):
        s = 1 << lvl
        same = (idx[:, None] // (2 * s)) == (idx[None, :] // (2 * s))
        hi_half, lo_half = (idx // s) % 2 == 1, (idx // s) % 2 == 0
        offdiag[0, lvl] = same & hi_half[:, None] & lo_half[None, :]
        offdiag[1, lvl] = same & lo_half[:, None] & hi_half[None, :]
    offdiag = np.tile(offdiag, (1, 1, 1, DN_GROUP))
    as16 = lambda a: jnp.asarray(a, dtype=BF16)
    return as16(ones_bd), as16(tri), as16(bd4), as16(bd8), as16(expand), as16(offdiag)


def _layer_weights(l, g_mix_pre, w_in, conv_w, dn_A_log, dn_dt_bias, dn_norm_g, da_lambda, da_subln_g,
                   sg_ln_g, sg_ln_b, sg_w, sg_b, w_branch, w_out, g_mix_post, g_mlp_pre, w_up, w_down,
                   g_mlp_post):
    sizes = (DN_WIDTH, DN_WIDTH, DN_WIDTH, DN_WIDTH, 4 * DN_HEADS, DA_WIDTH, DA_WIDTH, DA_WIDTH,
             2 * SG_WIDTH, N_BRANCH * D_MODEL)
    offs = np.concatenate([[0], np.cumsum(sizes)])
    w = w_in[l].astype(BF16)
    col = lambda a, b: w[:, int(offs[a]):int(offs[b])]
    w_gates = jnp.pad(col(4, 5), ((0, 0), (0, GATE_LANES - 4 * DN_HEADS)))
    in_ws = [col(0, 3), col(3, 4), w_gates, col(5, 6), col(6, 7), col(7, 8), col(8, 9), col(9, 10)]
    row = lambda v: v.astype(F32).reshape(1, -1)
    pad_lanes = lambda v: jnp.pad(v.astype(F32).reshape(1, -1), ((0, 0), (0, GATE_LANES - 2 * DN_HEADS)))
    return dict(
        g_pre=row(g_mix_pre[l]), in_ws=in_ws,
        conv_w=conv_w[l].astype(F32), alog=pad_lanes(dn_A_log[l]), dtb=pad_lanes(dn_dt_bias[l]),
        lam=da_lambda[l].astype(F32), subln=row(da_subln_g[l]),
        ln_g=row(sg_ln_g[l]), ln_b=row(sg_ln_b[l]),
        sg_wcat=jnp.transpose(sg_w[l], (1, 0, 2)).reshape(SG_CHUNK, SG_GROUPS * SG_CHUNK).astype(BF16),
        sg_bias=jnp.repeat(sg_b[l].astype(F32).T, SG_GROUP_DIM, axis=1),
        dn_g=jnp.tile(dn_norm_g[l].astype(F32), DN_HEADS)[None, :],
        wb=w_branch[l].astype(BF16), wout=w_out[l].astype(BF16), g_post=row(g_mix_post[l]),
        g_pre2=row(g_mlp_pre[l]), wup=w_up[l].astype(BF16), wdown=w_down[l].astype(BF16),
        g_post2=row(g_mlp_post[l]),
    )


def _lambda_init(layer):
    return 0.8 - 0.6 * math.exp(-0.3 * layer)


def _tile(n, pref):
    return pref if n % pref == 0 else n


def _run_layer(x, lw, lam_init, cos, sin, consts, batch, seq):
    ones_bd, tri, bd4, bd8, expand, offdiag = consts
    t = batch * seq
    dq, dk, dv, z, dgates, q_rot, k_rot, v_da, sg_uv, br = _inproj(
        x, lw["g_pre"], cos, sin, lw["in_ws"], (lw["conv_w"], lw["alog"], lw["dtb"], ones_bd, tri),
        seq, _tile(seq, 512))
    o_f, o_b = _dn_main(dq, dk, dv, dgates, expand, bd4, offdiag, batch, seq, _tile(seq // DN_CHUNK, 16))
    o_att = _diff_attn(q_rot, k_rot, v_da, lw["lam"], lw["subln"], batch, seq,
                       _tile(seq, 256), min(1024, seq // 2), lam_init)
    post_consts = (lw["ln_g"], lw["ln_b"], lw["sg_wcat"], lw["sg_bias"], bd8, lw["dn_g"], ones_bd, lw["wb"],
                   lw["wout"], lw["g_post"], lw["g_pre2"], lw["wup"], lw["wdown"], lw["g_post2"])
    return _post(x, o_f, o_b, z, o_att, sg_uv, br, post_consts, _tile(t, 512))


def kernel(x_prompt, x_sample, g_mix_pre, w_in, conv_w, dn_A_log, dn_dt_bias, dn_norm_g, da_lambda, da_subln_g,
           sg_ln_g, sg_ln_b, sg_w, sg_b, w_branch, w_out, g_mix_post, g_mlp_pre, w_up, w_down, g_mlp_post):
    params = (g_mix_pre, w_in, conv_w, dn_A_log, dn_dt_bias, dn_norm_g, da_lambda, da_subln_g, sg_ln_g, sg_ln_b,
              sg_w, sg_b, w_branch, w_out, g_mix_post, g_mlp_pre, w_up, w_down, g_mlp_post)
    depth = w_in.shape[0]
    consts = _np_consts()
    layers = [_layer_weights(l, *params) for l in range(depth)]
    cos, sin = _rope_tables(max(x_prompt.shape[1], x_sample.shape[1]))

    def trunk(x):
        batch, seq, _ = x.shape
        h = x.astype(F32).reshape(batch * seq, D_MODEL)
        for l in range(depth):
            h = _run_layer(h, layers[l], _lambda_init(l), cos, sin, consts, batch, seq)
        return h.reshape(batch, seq, D_MODEL)

    return trunk(x_prompt), trunk(x_sample)
```

```python
import functools
import math

import numpy as np
import jax
import jax.numpy as jnp
from jax import lax
from jax.experimental import pallas as pl
from jax.experimental.pallas import tpu as pltpu

F32 = jnp.float32
BF16 = jnp.bfloat16

D_MODEL = 1024
DN_HEADS = 8
DN_HEAD_DIM = 64
DN_WIDTH = DN_HEADS * DN_HEAD_DIM
DN_CONV = 5
DN_CHUNK = 64
DA_HEADS = 4
DA_HEAD_DIM = 64
DA_WIDTH = DA_HEADS * 2 * DA_HEAD_DIM
ROPE_THETA = 10000.0
SG_GROUPS = 8
SG_WIDTH = 512
SG_GROUP_DIM = SG_WIDTH // SG_GROUPS
SG_CHUNK = 128
N_BRANCH = 3
NORM_EPS = 1e-6
L2_EPS = 1e-6

LANES = 128
GATE_LANES = LANES
DN_GROUP = 4
DN_GROUP_LANES = DN_GROUP * DN_HEAD_DIM
DN_NGROUPS = DN_HEADS // DN_GROUP
HALO_ROWS = 16
DA_SOFTMAX_ROWS = 32
DN_UNROLL = 2

VMEM_LIMIT = 56 * 2**20


def _cparams(*sem):
    return pltpu.CompilerParams(dimension_semantics=sem, vmem_limit_bytes=VMEM_LIMIT)


def _const_spec(shape):
    nd = len(shape)
    return pl.BlockSpec(shape, lambda *_: (0,) * nd, pipeline_mode=pl.Buffered(1))


def _dot(a, b):
    return jnp.dot(a, b, preferred_element_type=F32)


def _dot_nt(a, b):
    return lax.dot_general(a, b, (((1,), (1,)), ((), ())), preferred_element_type=F32)


def _split2(x):
    hi = x.astype(BF16)
    lo = (x - hi.astype(F32)).astype(BF16)
    return hi, lo


def _rms(x, g):
    return x * lax.rsqrt(jnp.mean(x * x, axis=-1, keepdims=True) + NORM_EPS) * g


def _rope_body(inv_ref, cos_ref, sin_ref):
    tb = cos_ref.shape[0]
    row = lax.broadcasted_iota(jnp.int32, (tb, LANES), 0) + pl.program_id(0) * tb
    lane = lax.broadcasted_iota(jnp.int32, (tb, LANES), 1)
    ang = row.astype(F32) * inv_ref[...]
    half = DA_HEAD_DIM // 2
    sign = jnp.where(lane % DA_HEAD_DIM < half, -1.0, 1.0)
    cos_ref[...] = jnp.cos(ang)
    sin_ref[...] = jnp.sin(ang) * sign


def _rope_tables(seq):
    half = DA_HEAD_DIM // 2
    inv = jnp.power(ROPE_THETA, -jnp.arange(half, dtype=F32) / half)
    inv = jnp.tile(inv, LANES // half)[None, :]
    tb = min(seq, 1024)
    return pl.pallas_call(
        _rope_body,
        grid=(seq // tb,),
        in_specs=[pl.BlockSpec((1, LANES), lambda i: (0, 0))],
        out_specs=[pl.BlockSpec((tb, LANES), lambda i: (i, 0))] * 2,
        out_shape=[jax.ShapeDtypeStruct((seq, LANES), F32)] * 2,
        name="rope_tables",
        compiler_params=_cparams("parallel"),
    )(inv)


def _inproj_body(x_ref, xp_ref, xn_ref, g_ref, cos_ref, sin_ref, cw_ref, alog_ref, dtb_ref, ones_ref, tri_ref,
                 wqkv, wz, wg, wq, wk, wv, wsg, wbr,
                 odq, odk, odv, oz, og, oq, ok, ov, osg, obr, h_scr, ext_scr, *, tiles_per_seq):
    tm = x_ref.shape[0]
    halo = HALO_ROWS // 2
    h_scr[0:tm, :] = _rms(x_ref[...], g_ref[...]).astype(BF16)
    h_scr[tm:tm + halo, :] = _rms(xp_ref[...], g_ref[...]).astype(BF16)
    h_scr[tm + halo:tm + 2 * halo, :] = _rms(xn_ref[...], g_ref[...]).astype(BF16)
    chunk = 512

    def proj(w_ref, o_ref, epilogue=None):
        width = w_ref.shape[1]
        step = min(chunk, width)
        for c0 in range(0, width, step):
            y = _dot(h_scr[0:tm, :], w_ref[:, c0:c0 + step])
            if epilogue is not None:
                y = epilogue(y)
            o_ref[:, c0:c0 + step] = y.astype(o_ref.dtype)

    pos = pl.program_id(0) % tiles_per_seq
    keep_prev = (pos != 0).astype(F32)
    keep_next = (pos != tiles_per_seq - 1).astype(F32)
    for c0 in range(0, wqkv.shape[1], chunk):
        y = _dot(h_scr[...], wqkv[:, c0:c0 + chunk])
        ext_scr[halo:halo + tm, c0:c0 + chunk] = y[0:tm]
        ext_scr[0:halo, c0:c0 + chunk] = y[tm:tm + halo] * keep_prev
        ext_scr[halo + tm:2 * halo + tm, c0:c0 + chunk] = y[tm + halo:] * keep_next
    _dn_conv_silu(ext_scr, cw_ref, odv, tm)

    reps = DA_WIDTH // LANES
    cos = jnp.concatenate([cos_ref[...]] * reps, axis=1)
    sin = jnp.concatenate([sin_ref[...]] * reps, axis=1)
    lane = lax.broadcasted_iota(jnp.int32, (tm, DA_WIDTH), 1)
    first_half = lane % DA_HEAD_DIM < DA_HEAD_DIM // 2
    half = DA_HEAD_DIM // 2

    def rope(scale):
        def f(y):
            up = pltpu.roll(y, DA_WIDTH - half, axis=1)
            dn = pltpu.roll(y, half, axis=1)
            out = y * cos + jnp.where(first_half, up, dn) * sin
            return out * scale if scale != 1.0 else out
        return f

    proj(wz, oz)
    _dn_gates(_dot(h_scr[0:tm, :], wg[...]), alog_ref[...], dtb_ref[...], tri_ref[...], og)
    proj(wq, oq, rope(DA_HEAD_DIM ** -0.5 * math.log2(math.e)))
    proj(wk, ok, rope(1.0))
    proj(wv, ov)
    proj(wsg, osg)
    proj(wbr, obr)
    _dn_l2norm(ext_scr, ones_ref[...], (odq, odk), tm)


def _inproj(x, g, cos, sin, ws, dn_consts, seq, tm):
    t = x.shape[0]
    row = lambda w: pl.BlockSpec((tm, w), lambda i: (i, 0))
    tiles_per_seq = seq // tm
    halo = HALO_ROWS // 2
    hb = tm // halo
    nhalo = t // halo
    tab = pl.BlockSpec((tm, LANES), lambda i: (i % tiles_per_seq, 0))
    widths = [DN_WIDTH] * 3 + [w.shape[1] for w in ws[1:]]
    dtypes = [BF16, BF16, BF16, BF16, F32, BF16, BF16, BF16, BF16, BF16]
    return pl.pallas_call(
        functools.partial(_inproj_body, tiles_per_seq=tiles_per_seq),
        grid=(t // tm,),
        in_specs=[row(D_MODEL),
                  pl.BlockSpec((halo, D_MODEL), lambda i: (jnp.maximum(i * hb - 1, 0), 0)),
                  pl.BlockSpec((halo, D_MODEL), lambda i: (jnp.minimum((i + 1) * hb, nhalo - 1), 0)),
                  _const_spec((1, D_MODEL)), tab, tab]
                 + [_const_spec(c.shape) for c in dn_consts] + [_const_spec(w.shape) for w in ws],
        out_specs=[row(w) for w in widths],
        out_shape=[jax.ShapeDtypeStruct((t, w), d) for w, d in zip(widths, dtypes)],
        scratch_shapes=[pltpu.VMEM((tm + HALO_ROWS, D_MODEL), BF16),
                        pltpu.VMEM((tm + HALO_ROWS, ws[0].shape[1]), F32)],
        name="in_proj",
        compiler_params=_cparams("parallel"),
    )(x, x, x, g, cos, sin, *dn_consts, *ws)


def _dn_conv_silu(ext_scr, cw_ref, v_ref, tm):
    half = HALO_ROWS // 2
    pad = DN_CONV // 2
    for part in range(3):
        ls = slice(DN_WIDTH * part, DN_WIDTH * (part + 1))
        acc = None
        for j in range(DN_CONV):
            off = half - pad + j
            term = ext_scr[off:off + tm, ls] * cw_ref[j:j + 1, ls]
            acc = term if acc is None else acc + term
        y = acc * jax.nn.sigmoid(acc)
        if part < 2:
            ext_scr[half:half + tm, ls] = y
        else:
            v_ref[...] = y.astype(v_ref.dtype)


def _dn_l2norm(ext_scr, ones, o_refs, tm):
    half = HALO_ROWS // 2
    for part, o_ref in enumerate(o_refs):
        y = ext_scr[half:half + tm, DN_WIDTH * part:DN_WIDTH * (part + 1)]
        ss = _dot((y * y).astype(BF16), ones)
        y = y * lax.rsqrt(ss + L2_EPS)
        if part == 0:
            y = y * (DN_HEAD_DIM ** -0.5)
        o_ref[...] = y.astype(o_ref.dtype)


def _dn_gates(gl, alog, dtb, tri, go_ref):
    tm = gl.shape[0]
    lane = lax.broadcasted_iota(jnp.int32, (DN_CHUNK, GATE_LANES), 1)
    z = gl + dtb
    softplus = jnp.maximum(z, 0.0) + jnp.log1p(jnp.exp(-jnp.abs(z)))
    g = -jnp.exp(alog) * softplus
    beta = jax.nn.sigmoid(gl)
    for c in range(tm // DN_CHUNK):
        rows = slice(c * DN_CHUNK, (c + 1) * DN_CHUNK)
        gs = g[rows]
        hi = gs.astype(BF16)
        r1 = gs - hi.astype(F32)
        mid = r1.astype(BF16)
        lo = (r1 - mid.astype(F32)).astype(BF16)
        cs = _dot(tri, jnp.concatenate([hi, mid, lo], axis=1))
        cs = cs[:, :GATE_LANES] + cs[:, GATE_LANES:2 * GATE_LANES] + cs[:, 2 * GATE_LANES:]
        fwd, bwd = cs[:DN_CHUNK], cs[DN_CHUNK:]
        go_ref[rows, :] = jnp.where(lane < DN_HEADS, fwd, jnp.where(lane < 2 * DN_HEADS, bwd, beta[rows]))


def _blockdiag(x16, bd):
    return jnp.concatenate([x16] * DN_GROUP, axis=0) * bd


def _dn_parallel(probs, bd, eye):
    n = range(len(probs))
    q, k, v, gc, beta, strict, last_row, offdiag = (list(col) for col in zip(*probs))
    kf = [k[i].astype(F32) for i in n]
    kb = [kf[i] * beta[i] for i in n]
    pair = [_dot_nt(jnp.concatenate([kb[i].astype(BF16), q[i]], axis=0), _blockdiag(k[i], bd)) for i in n]
    yield
    gc_row = [jnp.sum(gc[i] * eye, axis=0, keepdims=True) for i in n]
    dec = [jnp.exp(jnp.where(strict[i] > 0.5, gc[i] - gc_row[i], -jnp.inf)) for i in n]
    qk = [(pair[i][DN_CHUNK:] * (dec[i] + eye)).astype(BF16) for i in n]

    gm = [pair[i][:DN_CHUNK] * dec[i] for i in n]
    for lvl in range(int(math.log2(DN_CHUNK))):
        g16 = [gm[i].astype(BF16) for i in n]
        gm = [gm[i] - _dot(g16[i] * offdiag[i][lvl], _blockdiag(g16[i], bd)) for i in n]
        yield

    egc = [jnp.exp(gc[i]) for i in n]
    vb = [v[i].astype(F32) * beta[i] for i in n]
    kbg = [kb[i] * egc[i] for i in n]
    uw = [jnp.concatenate([vb[i], kbg[i]], axis=1)
          - _dot(gm[i].astype(BF16), jnp.concatenate([_blockdiag(vb[i].astype(BF16), bd),
                                                      _blockdiag(kbg[i].astype(BF16), bd)], axis=1)) for i in n]
    g_last = [gc[i][last_row[i]:last_row[i] + 1, :] for i in n]
    out = []
    for i in n:
        kdec_t = (kf[i] * jnp.exp(g_last[i] - gc[i])).T.astype(BF16)
        out.append((uw[i][:, :DN_GROUP_LANES], uw[i][:, DN_GROUP_LANES:].astype(BF16),
                    (q[i].astype(F32) * egc[i]).astype(BF16), qk[i], kdec_t, jnp.exp(g_last[i])))
    return out


def _dn_sequential(s_refs, load_parts, store_out, bd):
    n = range(len(s_refs))
    bdf = bd.astype(F32)
    for step in range(DN_UNROLL):
        u, w16, qg16, qk16, kdec_t, eg_last = (list(col) for col in zip(*load_parts(step)))
        state = [s_refs[i][...] for i in n]
        ws = [_dot(jnp.concatenate([w16[i], qg16[i]], axis=0), state[i].astype(BF16)) for i in n]
        yield
        v_new = [(u[i] - ws[i][:DN_CHUNK]).astype(BF16) for i in n]
        o = [ws[i][DN_CHUNK:] + _dot(qk16[i], _blockdiag(v_new[i], bd)) for i in n]
        upd = [_dot(kdec_t[i], v_new[i]) for i in n]
        for i in n:
            s_refs[i][...] = (state[i] * eg_last[i] + upd[i]) * bdf
        store_out(step, o)
        yield


def _run_stages(gen):
    while True:
        try:
            next(gen)
        except StopIteration as stop:
            return stop.value


def _dn_body(qf, kf, vf, gf, qb, kb, vb, gb, exp_ref, bd_ref, od_ref, of_ref, ob_ref, state, gexp, *, cb):
    i = pl.program_id(1)

    @pl.when(i == 0)
    def _():
        state[...] = jnp.zeros_like(state)

    for d, g_ref in enumerate((gf, gb)):
        hi, lo = _split2(g_ref[...])
        e = _dot(jnp.concatenate([hi, lo], axis=0), exp_ref[d])
        rows = g_ref.shape[0]
        gexp[d] = e[:rows] + e[rows:]

    bd = bd_ref[...]
    ri = lax.broadcasted_iota(jnp.int32, (DN_CHUNK, DN_GROUP_LANES), 0)
    cj = lax.broadcasted_iota(jnp.int32, (DN_CHUNK, DN_GROUP_LANES), 1) % DN_CHUNK
    eye = (ri == cj).astype(F32)
    lower = (ri > cj).astype(F32)
    upper = (ri < cj).astype(F32)
    gl = DN_GROUP_LANES
    s_refs = [state.at[idx] for idx in range(2 * DN_NGROUPS)]

    ns = len(s_refs)

    def chunk_rows(trip, step, d):
        pos = trip * DN_UNROLL + step
        cc = pos if d == 0 else cb - 1 - pos
        return pl.ds(pl.multiple_of(cc * DN_CHUNK, DN_CHUNK), DN_CHUNK)

    def parallel(trip):
        probs = []
        for step in range(DN_UNROLL):
            for d, (q_ref, k_ref, v_ref) in enumerate(((qf, kf, vf), (qb, kb, vb))):
                rows = chunk_rows(trip, step, d)
                for grp in range(DN_NGROUPS):
                    ls = slice(grp * gl, (grp + 1) * gl)
                    gc = gexp[d, rows, grp * gl:(grp + 1) * gl]
                    beta = gexp[d, rows, (DN_NGROUPS + grp) * gl:(DN_NGROUPS + grp + 1) * gl]
                    probs.append((q_ref[rows, ls], k_ref[rows, ls], v_ref[rows, ls], gc, beta,
                                  lower if d == 0 else upper, DN_CHUNK - 1 if d == 0 else 0, od_ref.at[d]))
        return _dn_parallel(probs, bd, eye)

    def sequential(trip, parts):
        def load_parts(step):
            return parts[step * ns:(step + 1) * ns]

        def store_out(step, outs):
            for i, o in enumerate(outs):
                d, grp = divmod(i, DN_NGROUPS)
                o_ref = of_ref if d == 0 else ob_ref
                o_ref[chunk_rows(trip, step, d), grp * gl:(grp + 1) * gl] = o.astype(o_ref.dtype)

        return _dn_sequential(s_refs, load_parts, store_out, bd)

    def body(t, carry):
        parts = _run_stages(parallel(t))
        _run_stages(sequential(t, parts))
        return carry

    lax.fori_loop(0, cb // DN_UNROLL, body, 0)


def _dn_main(q, k, v, gates, expand, bd, offdiag, batch, seq, cb):
    t = q.shape[0]
    rows = cb * DN_CHUNK
    nb = seq // rows
    fwd =lambda w: pl.BlockSpec((rows, w), lambda b, i: (b * nb + i, 0))
    bwd = lambda w: pl.BlockSpec((rows, w), lambda b, i: (b * nb + nb - 1 - i, 0))
    return pl.pallas_call(
        functools.partial(_dn_body, cb=cb),
        grid=(batch, nb),
        in_specs=[fwd(DN_WIDTH)] * 3 + [fwd(GATE_LANES)] + [bwd(DN_WIDTH)] * 3 + [bwd(GATE_LANES)]
                 + [_const_spec(expand.shape), _const_spec(bd.shape), _const_spec(offdiag.shape)],
        out_specs=[fwd(DN_WIDTH), bwd(DN_WIDTH)],
        out_shape=[jax.ShapeDtypeStruct((t, DN_WIDTH), BF16)] * 2,
        scratch_shapes=[pltpu.VMEM((2 * DN_NGROUPS, DN_GROUP_LANES, DN_GROUP_LANES), F32),
                        pltpu.VMEM((2, rows, 2 * DN_NGROUPS * DN_GROUP_LANES), F32)],
        name="dn_delta_rule",
        compiler_params=_cparams("arbitrary", "arbitrary"),
    )(q, k, v, gates, q, k, v, gates, expand, bd, offdiag)


def _da_body(q_ref, k_ref, v_ref, lam_ref, g_ref, o_ref, s_scr, p_scr, m_scr, a_scr, acc_scr,
             *, tq, tk, lam_init):
    seq = k_ref.shape[0]
    nq = seq // tq
    nk = seq // tk
    assert nk % 2 == 0
    first_map = lax.broadcasted_iota(jnp.int32, (tq, LANES), 1) < DA_HEAD_DIM
    ones = jnp.ones((tk, LANES), BF16)
    lp = lam_ref[...]
    lam = (jnp.exp(jnp.sum(lp[0:1] * lp[1:2], axis=1, keepdims=True))
           - jnp.exp(jnp.sum(lp[2:3] * lp[3:4], axis=1, keepdims=True)) + lam_init)

    def key_rows(j):
        return pl.ds(pl.multiple_of(j * tk, tk), tk)

    def query_rows(qi):
        return pl.ds(pl.multiple_of(qi * tq, tq), tq)

    def qk_stage(qi, j, slot):
        q = q_ref[query_rows(qi), :]
        zero = jnp.zeros_like(q)
        q2 = jnp.concatenate([jnp.where(first_map, q, zero), jnp.where(first_map, zero, q)], axis=0)
        s_scr[slot] = _dot_nt(q2, k_ref[key_rows(j), :])

    def softmax_stage(slot, fresh=False):
        for r0 in range(0, 2 * tq, DA_SOFTMAX_ROWS):
            rows = slice(r0, r0 + DA_SOFTMAX_ROWS)
            s = s_scr[slot, rows, :]
            m_old = jnp.full((DA_SOFTMAX_ROWS, 1), -jnp.inf, F32) if fresh else m_scr[rows, :]
            m_new = jnp.maximum(m_old, jnp.max(s, axis=1, keepdims=True))
            p_scr[slot, rows, :] = jnp.exp2(s - m_new).astype(BF16)
            a_scr[slot, rows, :] = jnp.exp2(m_old - m_new)
            m_scr[rows, :] = m_new

    def pv_stage(j, slot):
        v_aug = jnp.concatenate([v_ref[key_rows(j), :], ones], axis=1)
        acc_scr[...] = a_scr[slot] * acc_scr[...] + _dot(p_scr[slot], v_aug)

    def finalize(qi):
        o = acc_scr[:, :LANES] / acc_scr[:, LANES:]
        od = o[:tq] - lam * o[tq:]
        o_ref[query_rows(qi), :] = (_rms(od, g_ref[...]) * (1.0 - lam_init)).astype(o_ref.dtype)

    acc_scr[...] = jnp.zeros(acc_scr.shape, F32)
    qk_stage(0, 0, 0)
    qk_stage(0, 1, 1)
    softmax_stage(0, fresh=True)

    def query_step(qi, carry):
        def pair(j):
            qk_stage(qi, j + 2, 0)
            softmax_stage(1)
            pv_stage(j, 0)
            qk_stage(qi, j + 3, 1)
            softmax_stage(0)
            pv_stage(j + 1, 1)

        for u in range(nk // 2 - 1):
            pair(2 * u)
        nxt = jnp.minimum(qi + 1, nq - 1)
        qk_stage(nxt, 0, 0)
        softmax_stage(1)
        pv_stage(nk - 2, 0)
        qk_stage(nxt, 1, 1)
        pv_stage(nk - 1, 1)
        finalize(qi)
        softmax_stage(0, fresh=True)
        return carry

    lax.fori_loop(0, nq, query_step, 0)


def _diff_attn(q, k, v, lam, g, batch, seq, tq, tk, lam_init):
    t = q.shape[0]
    spec = pl.BlockSpec((seq, LANES), lambda b, h: (b, h))
    return pl.pallas_call(
        functools.partial(_da_body, tq=tq, tk=tk, lam_init=lam_init),
        grid=(batch, DA_HEADS),
        in_specs=[spec, spec, spec,
                  pl.BlockSpec(lam.shape, lambda b, h: (0, 0)),
                  pl.BlockSpec(g.shape, lambda b, h: (0, 0))],
        out_specs=spec,
        out_shape=jax.ShapeDtypeStruct((t, DA_WIDTH), BF16),
        scratch_shapes=[pltpu.VMEM((2, 2 * tq, tk), F32),
                        pltpu.VMEM((2, 2 * tq, tk), BF16),
                        pltpu.VMEM((2 * tq, 1), F32),
                        pltpu.VMEM((2, 2 * tq, 1), F32),
                        pltpu.VMEM((2 * tq, 2 * LANES), F32)],
        name="diff_attn",
        compiler_params=_cparams("parallel", "parallel"),
    )(q, k, v, lam, g)


def _gelu_tanh(x):
    return 0.5 * x * (1.0 + jnp.tanh(math.sqrt(2.0 / math.pi) * (x + 0.044715 * (x * x * x))))


def _spatial_gating(uv_ref, lng_ref, lnb_ref, w_ref, bias_ref, bd_ref, o_scr):
    tm = uv_ref.shape[0]
    w = w_ref[...]
    bd = bd_ref[...]
    for c in range(tm // SG_CHUNK):
        rows = slice(c * SG_CHUNK, (c + 1) * SG_CHUNK)
        uv = _gelu_tanh(uv_ref[rows, :].astype(F32))
        u, v = uv[:, :SG_WIDTH], uv[:, SG_WIDTH:]
        mu = jnp.mean(v, axis=-1, keepdims=True)
        vc = v - mu
        vn = vc * lax.rsqrt(jnp.mean(vc * vc, axis=-1, keepdims=True) + NORM_EPS) * lng_ref[...] + lnb_ref[...]
        vbd = jnp.concatenate([vn.astype(BF16)] * SG_GROUPS, axis=0) * bd
        mixed = _dot(w, vbd) + bias_ref[...]
        o_scr[rows, :] = (u * mixed).astype(o_scr.dtype)


def _post_body(x_ref, of_ref, ob_ref, z_ref, oatt_ref, uv_ref, br_ref, lng_ref, lnb_ref, sgw_ref, sgb_ref,
               bd8_ref, dng_ref, ones_ref, wb_ref, wout_ref, gpost_ref, gpre2_ref, wup_ref, wdown_ref,
               gpost2_ref, y_ref, osg_scr):
    _spatial_gating(uv_ref, lng_ref, lnb_ref, sgw_ref, sgb_ref, bd8_ref, osg_scr)
    od = of_ref[...].astype(F32) + ob_ref[...].astype(F32)
    ms = _dot((od * od).astype(BF16), ones_ref[...]) * (1.0 / DN_HEAD_DIM)
    z = z_ref[...].astype(F32)
    o_a = od * lax.rsqrt(ms + NORM_EPS) * dng_ref[...] * (z * jax.nn.sigmoid(z))

    merged = None
    for idx, o_i in enumerate((o_a.astype(BF16), oatt_ref[...], osg_scr[...])):
        gate = jax.nn.sigmoid(br_ref[:, idx * D_MODEL:(idx + 1) * D_MODEL].astype(F32))
        term = gate * _dot(o_i, wb_ref[idx])
        merged = term if merged is None else merged + term
    mix = _dot(merged.astype(BF16), wout_ref[...])
    x1 = x_ref[...] + _rms(mix, gpost_ref[...])

    hm = _rms(x1, gpre2_ref[...]).astype(BF16)
    d_ff = wup_ref.shape[1]
    chunk = 1024
    f = None
    for c0 in range(0, d_ff, chunk):
        hid = jnp.maximum(_dot(hm, wup_ref[:, c0:c0 + chunk]), 0.0)
        term = _dot((hid * hid).astype(BF16), wdown_ref[c0:c0 + chunk, :])
        f = term if f is None else f + term
    y_ref[...] = x1 + _rms(f, gpost2_ref[...])


def _post(x, o_f, o_b, z, o_att, sg_uv, br, consts, tm):
    t = x.shape[0]
    row = lambda w: pl.BlockSpec((tm, w), lambda i: (i, 0))
    return pl.pallas_call(
        _post_body,
        grid=(t // tm,),
        in_specs=[row(D_MODEL), row(DN_WIDTH), row(DN_WIDTH), row(DN_WIDTH), row(DA_WIDTH), row(2 * SG_WIDTH),
                  row(N_BRANCH * D_MODEL)] + [_const_spec(c.shape) for c in consts],
        out_specs=row(D_MODEL),
        out_shape=jax.ShapeDtypeStruct((t, D_MODEL), F32),
        scratch_shapes=[pltpu.VMEM((tm, SG_WIDTH), BF16)],
        name="merge_mlp",
        compiler_params=_cparams("parallel"),
    )(x, o_f, o_b, z, o_att, sg_uv, br, *consts)


def _np_consts():
    ones_bd = np.kron(np.eye(DN_HEADS), np.ones((DN_HEAD_DIM, DN_HEAD_DIM))).astype(np.float32)
    idx = np.arange(DN_CHUNK)
    tri = np.concatenate([idx[:, None] >= idx[None, :], idx[:, None] <= idx[None, :]], axis=0).astype(np.float32)
    bd4 = np.kron(np.eye(DN_GROUP), np.ones((DN_CHUNK, DN_HEAD_DIM))).astype(np.float32)
    bd8 = np.kron(np.eye(SG_GROUPS), np.ones((SG_CHUNK, SG_GROUP_DIM))).astype(np.float32)
    expand = np.zeros((2, GATE_LANES, 2 * DN_NGROUPS * DN_GROUP_LANES), np.float32)
    for d in range(2):
        for kind in range(2):
            for h in range(DN_HEADS):
                src = kind * 2 * DN_HEADS + d * DN_HEADS + h
                dst = kind * DN_WIDTH + h * DN_HEAD_DIM
                expand[d, src, dst:dst + DN_HEAD_DIM] = 1.0
    levels = int(math.log2(DN_CHUNK))
    offdiag = np.zeros((2, levels, DN_CHUNK, DN_CHUNK), np.float32)
    for lvl in range(levels):
        s = 1 << lvl
        same = (idx[:, None] // (2 * s)) == (idx[None, :] // (2 * s))
        hi_half, lo_half = (idx // s) % 2 == 1, (idx // s) % 2 == 0
        offdiag[0, lvl] = same & hi_half[:, None] & lo_half[None, :]
        offdiag[1, lvl] = same & lo_half[:, None] & hi_half[None, :]
    offdiag = np.tile(offdiag, (1, 1, 1, DN_GROUP))
    as16 = lambda a: jnp.asarray(a, dtype=BF16)
    return as16(ones_bd), as16(tri), as16(bd4), as16(bd8), as16(expand), as16(offdiag)


def _layer_weights(l, g_mix_pre, w_in, conv_w, dn_A_log, dn_dt_bias, dn_norm_g, da_lambda, da_subln_g,
                   sg_ln_g, sg_ln_b, sg_w, sg_b, w_branch, w_out, g_mix_post, g_mlp_pre, w_up, w_down,
                   g_mlp_post):
    sizes = (DN_WIDTH, DN_WIDTH, DN_WIDTH, DN_WIDTH, 4 * DN_HEADS, DA_WIDTH, DA_WIDTH, DA_WIDTH,
             2 * SG_WIDTH, N_BRANCH * D_MODEL)
    offs = np.concatenate([[0], np.cumsum(sizes)])
    w = w_in[l].astype(BF16)
    col = lambda a, b: w[:, int(offs[a]):int(offs[b])]
    w_gates = jnp.pad(col(4, 5), ((0, 0), (0, GATE_LANES - 4 * DN_HEADS)))
    in_ws = [col(0, 3), col(3, 4), w_gates, col(5, 6), col(6, 7), col(7, 8), col(8, 9), col(9, 10)]
    row = lambda v: v.astype(F32).reshape(1, -1)
    pad_lanes = lambda v: jnp.pad(v.astype(F32).reshape(1, -1), ((0, 0), (0, GATE_LANES - 2 * DN_HEADS)))
    return dict(
        g_pre=row(g_mix_pre[l]), in_ws=in_ws,
        conv_w=conv_w[l].astype(F32), alog=pad_lanes(dn_A_log[l]), dtb=pad_lanes(dn_dt_bias[l]),
        lam=da_lambda[l].astype(F32), subln=row(da_subln_g[l]),
        ln_g=row(sg_ln_g[l]), ln_b=row(sg_ln_b[l]),
        sg_wcat=jnp.transpose(sg_w[l], (1, 0, 2)).reshape(SG_CHUNK, SG_GROUPS * SG_CHUNK).astype(BF16),
        sg_bias=jnp.repeat(sg_b[l].astype(F32).T, SG_GROUP_DIM, axis=1),
        dn_g=jnp.tile(dn_norm_g[l].astype(F32), DN_HEADS)[None, :],
        wb=w_branch[l].astype(BF16), wout=w_out[l].astype(BF16), g_post=row(g_mix_post[l]),
        g_pre2=row(g_mlp_pre[l]), wup=w_up[l].astype(BF16), wdown=w_down[l].astype(BF16),
        g_post2=row(g_mlp_post[l]),
    )


def _lambda_init(layer):
    return 0.8 - 0.6 * math.exp(-0.3 * layer)


def _tile(n, pref):
    return pref if n % pref == 0 else n


def _run_layer(x, lw, lam_init, cos, sin, consts, batch, seq):
    ones_bd, tri, bd4, bd8, expand, offdiag = consts
    t = batch * seq
    dq, dk, dv, z, dgates, q_rot, k_rot, v_da, sg_uv, br = _inproj(
        x, lw["g_pre"], cos, sin, lw["in_ws"], (lw["conv_w"], lw["alog"], lw["dtb"], ones_bd, tri),
        seq, _tile(seq, 512))
    o_f, o_b = _dn_main(dq, dk, dv, dgates, expand, bd4, offdiag, batch, seq, _tile(seq // DN_CHUNK, 16))
    o_att = _diff_attn(q_rot, k_rot, v_da, lw["lam"], lw["subln"], batch, seq,
                       _tile(seq, 256), min(1024, seq // 2), lam_init)
    post_consts = (lw["ln_g"], lw["ln_b"], lw["sg_wcat"], lw["sg_bias"], bd8, lw["dn_g"], ones_bd, lw["wb"],
                   lw["wout"], lw["g_post"], lw["g_pre2"], lw["wup"], lw["wdown"], lw["g_post2"])
    return _post(x, o_f, o_b, z, o_att, sg_uv, br, post_consts, _tile(t, 512))


def kernel(x_prompt, x_sample, g_mix_pre, w_in, conv_w, dn_A_log, dn_dt_bias, dn_norm_g, da_lambda, da_subln_g,
           sg_ln_g, sg_ln_b, sg_w, sg_b, w_branch, w_out, g_mix_post, g_mlp_pre, w_up, w_down, g_mlp_post):
    params = (g_mix_pre, w_in, conv_w, dn_A_log, dn_dt_bias, dn_norm_g, da_lambda, da_subln_g, sg_ln_g, sg_ln_b,
              sg_w, sg_b, w_branch, w_out, g_mix_post, g_mlp_pre, w_up, w_down, g_mlp_post)
    depth = w_in.shape[0]
    consts = _np_consts()
    layers = [_layer_weights(l, *params) for l in range(depth)]
    cos, sin = _rope_tables(max(x_prompt.shape[1], x_sample.shape[1]))

    def trunk(x):
        batch, seq, _ = x.shape
        h = x.astype(F32).reshape(batch * seq, D_MODEL)
        for l in range(depth):
            h = _run_layer(h, layers[l], _lambda_init(l), cos, sin, consts, batch, seq)
        return h.reshape(batch, seq, D_MODEL)

    return trunk(x_prompt), trunk(x_sample)
```
